```python
import math
import jax, jax.numpy as jnp
from jax import lax
import numpy as np

D_MODEL = 1024
BATCH = 4
SEQ = 4096
DEPTH = 1
DEC_BATCH = 4
DEC_SEQ = 8192
PAST_LEN = 128

ATTN_GROUPS = ((128, 1), (512, 4), (2048, 16))
N_GROUPS = 3
ATTN_HEADS_PER_GROUP = 8
ATTN_HEAD_DIM = 64
ATTN_WIDTH = N_GROUPS * ATTN_HEADS_PER_GROUP * ATTN_HEAD_DIM
ATTN_OUT_WIDTH = ATTN_HEADS_PER_GROUP * ATTN_HEAD_DIM
HALF_TAPS = ATTN_GROUPS[0][0] // (2 * ATTN_GROUPS[0][1])
ATTN_BLOCK = HALF_TAPS
ALIBI_SLOPES = tuple(2.0 ** (-8.0 * (j + 1) / ATTN_HEADS_PER_GROUP) for j in range(ATTN_HEADS_PER_GROUP))

HGRN_EXPAND = 128
HGRN_HEADS = D_MODEL // HGRN_EXPAND
HGRN_KEY_DIM = HGRN_EXPAND
HGRN_VAL_DIM = D_MODEL // HGRN_HEADS
HGRN_KEY_WIDTH = HGRN_HEADS * HGRN_KEY_DIM
HGRN_WIDTH = HGRN_HEADS * HGRN_VAL_DIM
HGRN_CHUNK = 64

D_FF = 2816
RMS_EPS = 1e-6
NEG_INF = -1e30

IN_SIZES = (ATTN_WIDTH, ATTN_WIDTH, ATTN_WIDTH,
            HGRN_KEY_WIDTH, HGRN_KEY_WIDTH, HGRN_KEY_WIDTH,
            HGRN_WIDTH, HGRN_WIDTH, D_MODEL, D_MODEL)
IN_WIDTH = sum(IN_SIZES)
IN_SPLITS = tuple(int(v) for v in np.cumsum(IN_SIZES)[:-1])

kernel_name = "dilated_attn_hgrn2_gated_encoder"


def rms_norm(x, g):
    xf = x.astype(jnp.float32)
    y = xf * lax.rsqrt(jnp.mean(xf * xf, axis=-1, keepdims=True) + RMS_EPS)
    return (y * g.astype(jnp.float32)).astype(x.dtype)


def swiglu(x, w_gu, w_down):
    a, b = jnp.split(x @ w_gu, 2, axis=-1)
    return (jax.nn.silu(a) * b) @ w_down


def dilated_window_attn(q, k, v, dilation, slopes):
    B, S, H, E = q.shape
    f32 = jnp.float32
    L = S // dilation
    nb = -(-L // ATTN_BLOCK)
    Lp = nb * ATTN_BLOCK

    def to_sub(t):
        return t.reshape(B, L, dilation, H, E).transpose(0, 2, 1, 3, 4).astype(f32)

    qs = jnp.pad(to_sub(q), ((0, 0), (0, 0), (0, Lp - L), (0, 0), (0, 0)))
    pad_kv = ((0, 0), (0, 0), (ATTN_BLOCK, Lp - L + ATTN_BLOCK), (0, 0), (0, 0))
    ks = jnp.pad(to_sub(k), pad_kv)
    vs = jnp.pad(to_sub(v), pad_kv)

    def band(t):
        return jnp.concatenate(
            [t[:, :, j * ATTN_BLOCK: j * ATTN_BLOCK + Lp].reshape(B, dilation, nb, ATTN_BLOCK, H, E)
             for j in range(3)], axis=3)

    qb = qs.reshape(B, dilation, nb, ATTN_BLOCK, H, E)
    kb, vb = band(ks), band(vs)
    s = jnp.einsum('bdnqhe,bdnkhe->bdnhqk', qb, kb) * (1.0 / math.sqrt(E))

    qi = jnp.arange(ATTN_BLOCK)
    kj = jnp.arange(3 * ATTN_BLOCK) - ATTN_BLOCK
    rel = kj[None, :] - qi[:, None]
    key_pos = jnp.arange(nb)[:, None] * ATTN_BLOCK + kj[None, :]
    valid = (jnp.abs(rel) <= HALF_TAPS)[None] & ((key_pos >= 0) & (key_pos < L))[:, None, :]
    bias = -(slopes * dilation)[:, None, None] * jnp.abs(rel).astype(f32)[None]
    s = jnp.where(valid[:, None], s + bias[None], NEG_INF)

    m = jnp.max(s, axis=-1, keepdims=True)
    p = jnp.exp(s - m)
    den = jnp.sum(p, axis=-1, keepdims=True)
    o = jnp.einsum('bdnhqk,bdnkhe->bdnqhe', p / den, vb)
    lse = (m + jnp.log(den))[..., 0]

    o = o.reshape(B, dilation, Lp, H, E)[:, :, :L].transpose(0, 2, 1, 3, 4).reshape(B, S, H, E)
    lse = lse.transpose(0, 1, 2, 4, 3).reshape(B, dilation, Lp, H)[:, :, :L]
    lse = lse.transpose(0, 2, 1, 3).reshape(B, S, H)
    return o, lse


def dilated_mixture_attention(a_q, a_k, a_v):
    B, S, _ = a_q.shape
    shp = (B, S, N_GROUPS, ATTN_HEADS_PER_GROUP, ATTN_HEAD_DIM)
    q, k, v = a_q.reshape(shp), a_k.reshape(shp), a_v.reshape(shp)
    slopes = jnp.asarray(ALIBI_SLOPES, jnp.float32)
    outs, lses = [], []
    for g, (_, dil) in enumerate(ATTN_GROUPS):
        o, lse = dilated_window_attn(q[:, :, g], k[:, :, g], v[:, :, g], dil, slopes)
        outs.append(o)
        lses.append(lse)
    w = jax.nn.softmax(jnp.stack(lses, axis=0), axis=0)
    o = jnp.sum(w[..., None] * jnp.stack(outs, axis=0), axis=0)
    return o.reshape(B, S, ATTN_OUT_WIDTH).astype(a_q.dtype)


def hgrn2_chunk_scan(q, k, log_f, v):
    B, S, H, K = q.shape
    V = v.shape[-1]
    C = HGRN_CHUNK
    nc = S // C

    def chunks(t):
        return t.reshape(B, nc, C, H, t.shape[-1]).transpose(1, 0, 3, 2, 4)

    qc, kc, gc, vc = chunks(q), chunks(k), chunks(log_f), chunks(v)
    b = jnp.cumsum(gc, axis=3)
    b_last = b[:, :, :, -1:]
    q_t = qc * jnp.exp(b)
    k_t = kc * jnp.exp(-b)
    k_end = kc * jnp.exp(b_last - b)
    causal = jnp.tril(jnp.ones((C, C), dtype=bool))
    A = jnp.where(causal, jnp.einsum('nbhtk,nbhsk->nbhts', q_t, k_t), 0.0)
    o_intra = jnp.einsum('nbhts,nbhsv->nbhtv', A, vc)

    def step(state, inp):
        q_i, k_i, v_i, dec = inp
        o = jnp.einsum('bhtk,bhkv->bhtv', q_i, state)
        state = state * dec[:, :, 0, :, None] + jnp.einsum('bhsk,bhsv->bhkv', k_i, v_i)
        return state, o

    state0 = jnp.zeros((B, H, K, V), jnp.float32)
    _, o_inter = lax.scan(step, state0, (q_t, k_end, vc, jnp.exp(b_last)))
    o = o_intra + o_inter
    return o.transpose(1, 0, 3, 2, 4).reshape(B, S, H, V)


def hgrn2_bidirectional(hq, hf_f, hf_b, hi, hg, lb_f, lb_b, norm_g):
    B, S, _ = hq.shape
    f32 = jnp.float32
    kshape = (B, S, HGRN_HEADS, HGRN_KEY_DIM)
    q = jax.nn.silu(hq.astype(f32)).reshape(kshape)
    v = hi.astype(f32).reshape(B, S, HGRN_HEADS, HGRN_VAL_DIM)

    def gates(raw, lb):
        f = lb + (1.0 - lb) * jax.nn.sigmoid(raw.astype(f32))
        return (1.0 - f).reshape(kshape), jnp.log(f).reshape(kshape)

    k_f, lf_f = gates(hf_f, lb_f)
    k_b, lf_b = gates(hf_b, lb_b)
    rev = lambda t: jnp.flip(t, axis=1)
    o = hgrn2_chunk_scan(q, k_f, lf_f, v) + rev(hgrn2_chunk_scan(rev(q), rev(k_b), rev(lf_b), rev(v)))
    o = o * lax.rsqrt(jnp.mean(o * o, axis=-1, keepdims=True) + RMS_EPS)
    o = o.reshape(B, S, HGRN_WIDTH) * norm_g.astype(f32) * jax.nn.silu(hg.astype(f32))
    return o.astype(hq.dtype)


def encoder_layer(x, l, ffn1_norm, ffn1_w_gu, ffn1_w_down, mix_norm, w_in,
                  hgrn_lb_fwd, hgrn_lb_bwd, hgrn_norm, w_branch_a, w_branch_b, w_out,
                  ffn2_norm, ffn2_w_gu, ffn2_w_down):
    x = x + 0.5 * swiglu(rms_norm(x, ffn1_norm[l]), ffn1_w_gu[l], ffn1_w_down[l])
    h = rms_norm(x, mix_norm[l])
    proj = h @ w_in[l]
    a_q, a_k, a_v, h_q, h_ff, h_fb, h_i, h_g, g_a, g_b = jnp.split(proj, IN_SPLITS, axis=-1)
    ya = dilated_mixture_attention(a_q, a_k, a_v)
    lb_f = jnp.cumsum(jax.nn.softmax(hgrn_lb_fwd.astype(jnp.float32), axis=0), axis=0)[l]
    lb_b = jnp.cumsum(jax.nn.softmax(hgrn_lb_bwd.astype(jnp.float32), axis=0), axis=0)[l]
    yb = hgrn2_bidirectional(h_q, h_ff, h_fb, h_i, h_g, lb_f, lb_b, hgrn_norm[l])
    merged = jax.nn.sigmoid(g_a) * (ya @ w_branch_a[l]) + jax.nn.sigmoid(g_b) * (yb @ w_branch_b[l])
    x = x + merged @ w_out[l]
    x = x + 0.5 * swiglu(rms_norm(x, ffn2_norm[l]), ffn2_w_gu[l], ffn2_w_down[l])
    return x


def setup_inputs(seed: int = 0) -> dict:
    key = jax.random.key(seed)
    ks = jax.random.split(key, 20)
    f32 = jnp.float32
    nrm = lambda k, shape, fan_in: jax.random.normal(k, shape, f32) * (fan_in ** -0.5)
    gain = lambda k, shape: 1.0 + 0.01 * jax.random.normal(k, shape, f32)
    return {
        "x_prompt": jax.random.normal(ks[0], (BATCH, SEQ, D_MODEL), f32),
        "x_sample": jax.random.normal(ks[1], (DEC_BATCH, DEC_SEQ, D_MODEL), f32),
        "ffn1_norm": gain(ks[2], (DEPTH, D_MODEL)),
        "ffn1_w_gu": nrm(ks[3], (DEPTH, D_MODEL, 2 * D_FF), D_MODEL),
        "ffn1_w_down": nrm(ks[4], (DEPTH, D_FF, D_MODEL), D_FF),
        "mix_norm": gain(ks[5], (DEPTH, D_MODEL)),
        "w_in": nrm(ks[6], (DEPTH, D_MODEL, IN_WIDTH), D_MODEL),
        "hgrn_lb_fwd": 0.1 * jax.random.normal(ks[7], (DEPTH + 1, HGRN_KEY_WIDTH), f32),
        "hgrn_lb_bwd": 0.1 * jax.random.normal(ks[8], (DEPTH + 1, HGRN_KEY_WIDTH), f32),
        "hgrn_norm": gain(ks[9], (DEPTH, HGRN_WIDTH)),
        "w_branch_a": nrm(ks[10], (DEPTH, ATTN_OUT_WIDTH, D_MODEL), ATTN_OUT_WIDTH),
        "w_branch_b": nrm(ks[11], (DEPTH, HGRN_WIDTH, D_MODEL), HGRN_WIDTH),
        "w_out": nrm(ks[12], (DEPTH, D_MODEL, D_MODEL), D_MODEL),
        "ffn2_norm": gain(ks[13], (DEPTH, D_MODEL)),
        "ffn2_w_gu": nrm(ks[14], (DEPTH, D_MODEL, 2 * D_FF), D_MODEL),
        "ffn2_w_down": nrm(ks[15], (DEPTH, D_FF, D_MODEL), D_FF),
        "final_norm": gain(ks[16], (D_MODEL,)),
    }


def reference(x_prompt, x_sample, ffn1_norm, ffn1_w_gu, ffn1_w_down, mix_norm, w_in,
              hgrn_lb_fwd, hgrn_lb_bwd, hgrn_norm, w_branch_a, w_branch_b, w_out,
              ffn2_norm, ffn2_w_gu, ffn2_w_down, final_norm):
    def trunk(x):
        for l in range(DEPTH):
            x = encoder_layer(x, l, ffn1_norm, ffn1_w_gu, ffn1_w_down, mix_norm, w_in,
                              hgrn_lb_fwd, hgrn_lb_bwd, hgrn_norm, w_branch_a, w_branch_b, w_out,
                              ffn2_norm, ffn2_w_gu, ffn2_w_down)
        return rms_norm(x, final_norm)

    y_prompt = trunk(x_prompt)
    y_sample = trunk(x_sample)
    return (y_prompt, y_sample)
```

```python
import functools
import math

import jax
import jax.numpy as jnp
from jax import lax
from jax.experimental import pallas as pl
from jax.experimental.pallas import tpu as pltpu

D_MODEL = 1024
D_FF = 2816
RMS_EPS = 1e-6
MASK_BIAS = -1e30

ATTN_DILATIONS = (1, 4, 16)
ATTN_HEADS = 8
ATTN_HEAD_DIM = 64
ATTN_GROUP_WIDTH = ATTN_HEADS * ATTN_HEAD_DIM
HALF_TAPS = 64
ATTN_SUB = 128
ATTN_WIN = ATTN_SUB + 2 * HALF_TAPS

HGRN_HEADS = 8
HGRN_DIM = 128
HGRN_CHUNK = 64
HGRN_TILE = 256

FF_CHUNK = 256
N_FF_CHUNKS = D_FF // FF_CHUNK

COL_GATES = 0
COL_HQ = 2048
COL_HFF = 3072
COL_HFB = 4096
COL_HI = 5120
COL_HG = 6144
COL_ATTN = 7168
ATTN_QKV_WIDTH = 3 * ATTN_GROUP_WIDTH
MAIN_WIDTH = COL_ATTN + ATTN_QKV_WIDTH
PROJ_CHUNK = 512

VMEM_LIMIT = 56 * 1024 * 1024

_NT = (((1,), (1,)), ((), ()))
_TN = (((0,), (0,)), ((), ()))


def _resident(shape):
    zeros = (0,) * len(shape)
    return pl.BlockSpec(shape, lambda *_: zeros, pipeline_mode=pl.Buffered(1))


def _rms(x, g):
    return x * lax.rsqrt(jnp.mean(x * x, axis=-1, keepdims=True) + RMS_EPS) * g


def _swiglu_into(acc_ref, h_ref, wg_ref, wu_ref, wd_ref):
    acc_ref[...] = jnp.zeros_like(acc_ref)

    def chunk(c, carry):
        h = h_ref[...]
        a = jnp.dot(h, wg_ref[c], preferred_element_type=jnp.float32)
        b = jnp.dot(h, wu_ref[c], preferred_element_type=jnp.float32)
        act = (a * jax.nn.sigmoid(a) * b).astype(jnp.bfloat16)
        acc_ref[...] += jnp.dot(act, wd_ref[c], preferred_element_type=jnp.float32)
        return carry

    lax.fori_loop(0, N_FF_CHUNKS, chunk, 0)


def _ffn1_kernel(x_ref, g1_ref, gm_ref, wg_ref, wu_ref, wd_ref, x1_ref, h_ref, hs_ref, acc_ref):
    x = x_ref[...]
    hs_ref[...] = _rms(x, g1_ref[...]).astype(jnp.bfloat16)
    _swiglu_into(acc_ref, hs_ref, wg_ref, wu_ref, wd_ref)
    x1 = x + 0.5 * acc_ref[...]
    x1_ref[...] = x1
    h_ref[...] = _rms(x1, gm_ref[...]).astype(jnp.bfloat16)


def _ffn1(x2d, g1, gm, wg, wu, wd, tm):
    n = x2d.shape[0]
    row = lambda i: (i, 0)
    return pl.pallas_call(
        _ffn1_kernel,
        grid=(n // tm,),
        in_specs=[
            pl.BlockSpec((tm, D_MODEL), row),
            _resident((1, D_MODEL)),
            _resident((1, D_MODEL)),
            _resident(wg.shape),
            _resident(wu.shape),
            _resident(wd.shape),
        ],
        out_specs=[pl.BlockSpec((tm, D_MODEL), row), pl.BlockSpec((tm, D_MODEL), row)],
        out_shape=[
            jax.ShapeDtypeStruct((n, D_MODEL), jnp.float32),
            jax.ShapeDtypeStruct((n, D_MODEL), jnp.bfloat16),
        ],
        scratch_shapes=[
            pltpu.VMEM((tm, D_MODEL), jnp.bfloat16),
            pltpu.VMEM((tm, D_MODEL), jnp.float32),
        ],
        compiler_params=pltpu.CompilerParams(
            dimension_semantics=("arbitrary",), vmem_limit_bytes=VMEM_LIMIT),
        name="ffn1",
    )(x2d, g1, gm, wg, wu, wd)


def _proj_kernel(h_ref, w_ref, p_ref):
    h = h_ref[...]
    for j in range(w_ref.shape[1] // PROJ_CHUNK):
        cols = slice(j * PROJ_CHUNK, (j + 1) * PROJ_CHUNK)
        p_ref[:, cols] = jnp.dot(
            h, w_ref[:, cols], preferred_element_type=jnp.float32).astype(jnp.bfloat16)


def _proj(h2d, w, tm, name):
    n = h2d.shape[0]
    width = w.shape[1]
    row = lambda i: (i, 0)
    return pl.pallas_call(
        _proj_kernel,
        grid=(n // tm,),
        in_specs=[pl.BlockSpec((tm, D_MODEL), row), _resident(w.shape)],
        out_specs=pl.BlockSpec((tm, width), row),
        out_shape=jax.ShapeDtypeStruct((n, width), jnp.bfloat16),
        compiler_params=pltpu.CompilerParams(
            dimension_semantics=("arbitrary",), vmem_limit_bytes=VMEM_LIMIT),
        name=name,
    )(h2d, w)


def _attn_kernel(q_ref, kp_ref, kc_ref, kn_ref, vp_ref, vc_ref, vn_ref, o_ref, lse_ref,
                 *, dilation, sub_len, tq):
    q0 = pl.program_id(1) * tq
    kwin = jnp.concatenate([kp_ref[0], kc_ref[0], kn_ref[0]], axis=0)
    vwin = jnp.concatenate([vp_ref[0], vc_ref[0], vn_ref[0]], axis=0)

    row = lax.broadcasted_iota(jnp.int32, (ATTN_SUB, ATTN_WIN), 0)
    col = lax.broadcasted_iota(jnp.int32, (ATTN_SUB, ATTN_WIN), 1)
    rel = jnp.abs(col - HALF_TAPS - row)
    lane = lax.broadcasted_iota(jnp.int32, (ATTN_SUB, 2 * ATTN_HEAD_DIM), 1)
    first_head = lane < ATTN_HEAD_DIM
    ones = jnp.ones((ATTN_WIN, 2 * ATTN_HEAD_DIM), jnp.bfloat16)

    for sub in range(tq // ATTN_SUB):
        r0 = sub * ATTN_SUB
        kpos = q0 + (r0 - HALF_TAPS) + col
        valid = (rel <= HALF_TAPS) & (kpos >= 0) & (kpos < sub_len)
        base = jnp.where(valid, (-dilation) * rel.astype(jnp.float32), MASK_BIAS)
        for pair in range(ATTN_HEADS // 2):
            lanes = slice(pair * 128, (pair + 1) * 128)
            qp = q_ref[0, r0:r0 + ATTN_SUB, lanes]
            kw = kwin[r0:r0 + ATTN_WIN, lanes]
            vw = vwin[r0:r0 + ATTN_WIN, lanes]
            zero = jnp.zeros_like(qp)
            q2 = jnp.concatenate(
                [jnp.where(first_head, qp, zero), jnp.where(first_head, zero, qp)], axis=0)
            s = lax.dot_general(q2, kw, _NT, preferred_element_type=jnp.float32)
            halves = []
            for half in range(2):
                slope = 2.0 ** (-(2 * pair + half + 1))
                sh = s[half * ATTN_SUB:(half + 1) * ATTN_SUB] + slope * base
                m = jnp.max(sh, axis=-1, keepdims=True)
                halves.append((m, jnp.exp(sh - m).astype(jnp.bfloat16)))
            p2 = jnp.concatenate([halves[0][1], halves[1][1]], axis=0)
            vext = jnp.concatenate([vw, ones], axis=1)
            o2 = jnp.dot(p2, vext, preferred_element_type=jnp.float32)
            outs, lses = [], []
            for half in range(2):
                rows = slice(half * ATTN_SUB, (half + 1) * ATTN_SUB)
                num = o2[rows, :128]
                den = o2[rows, 128:]
                outs.append(num / den)
                lses.append(halves[half][0] + jnp.log(den))
            o_ref[0, r0:r0 + ATTN_SUB, lanes] = jnp.where(first_head, outs[0], outs[1])
            lse_ref[0, r0:r0 + ATTN_SUB, lanes] = jnp.where(first_head, lses[0], lses[1])


def _attention(qkv, col0, dilation):
    nsub, sub_len, _ = qkv.shape
    tq = 256 if sub_len >= 256 else ATTN_SUB
    halo_per_tile = tq // HALF_TAPS
    last_halo = sub_len // HALF_TAPS - 1

    def cur(which):
        return lambda si, i: (si, i, col0 + which)

    def prev(which):
        return lambda si, i: (si, jnp.maximum(i * halo_per_tile - 1, 0), col0 + which)

    def nxt(which):
        return lambda si, i: (si, jnp.minimum((i + 1) * halo_per_tile, last_halo), col0 + which)

    main = lambda which: pl.BlockSpec((1, tq, ATTN_GROUP_WIDTH), cur(which))
    halo = lambda fn, which: pl.BlockSpec((1, HALF_TAPS, ATTN_GROUP_WIDTH), fn(which))
    out_spec = pl.BlockSpec((1, tq, ATTN_GROUP_WIDTH), lambda si, i: (si, i, 0))
    out_sds = jax.ShapeDtypeStruct((nsub, sub_len, ATTN_GROUP_WIDTH), jnp.float32)
    return pl.pallas_call(
        functools.partial(_attn_kernel, dilation=dilation, sub_len=sub_len, tq=tq),
        grid=(nsub, sub_len // tq),
        in_specs=[main(0), halo(prev, 1), main(1), halo(nxt, 1), halo(prev, 2), main(2), halo(nxt, 2)],
        out_specs=[out_spec, out_spec],
        out_shape=[out_sds, out_sds],
        compiler_params=pltpu.CompilerParams(
            dimension_semantics=("arbitrary", "arbitrary"), vmem_limit_bytes=VMEM_LIMIT),
        name=f"attn_d{dilation}",
    )(qkv, qkv, qkv, qkv, qkv, qkv, qkv)


def _split3(x):
    hi = x.astype(jnp.bfloat16)
    r1 = x - hi.astype(jnp.float32)
    mid = r1.astype(jnp.bfloat16)
    lo = (r1 - mid.astype(jnp.float32)).astype(jnp.bfloat16)
    return hi, mid, lo


def _hgrn_direction(rows, tri_ref, hq_ref, hf_ref, hi_ref, lb, state_ref, acc_ref, *, reverse):
    f32, bf16 = jnp.float32, jnp.bfloat16
    z = hf_ref[0, rows, :].astype(f32)
    f = lb + (1.0 - lb) * jax.nn.sigmoid(z)
    k = 1.0 - f
    hi3, mid3, lo3 = _split3(jnp.log(f))
    tri = tri_ref[...]
    cum = lambda t: jnp.dot(tri, t, preferred_element_type=f32)
    bcum = (cum(lo3) + cum(mid3)) + cum(hi3)
    hq = hq_ref[0, rows, :].astype(f32)
    qs = hq * jax.nn.sigmoid(hq)
    v = hi_ref[0, rows, :]

    tpos = lax.broadcasted_iota(jnp.int32, (HGRN_CHUNK, HGRN_CHUNK), 0)
    spos = lax.broadcasted_iota(jnp.int32, (HGRN_CHUNK, HGRN_CHUNK), 1)
    seen = (tpos <= spos) if reverse else (tpos >= spos)

    state = state_ref[...]
    n_chunks = HGRN_TILE // HGRN_CHUNK
    order = range(n_chunks - 1, -1, -1) if reverse else range(n_chunks)
    outs = [None] * n_chunks
    for c in order:
        cs = slice(c * HGRN_CHUNK, (c + 1) * HGRN_CHUNK)
        bc = bcum[cs]
        edge = c * HGRN_CHUNK if reverse else (c + 1) * HGRN_CHUNK - 1
        b_last = bcum[edge:edge + 1]
        q_t = (qs[cs] * jnp.exp(bc)).astype(bf16)
        k_t = (k[cs] * jnp.exp(-bc)).astype(bf16)
        k_end = (k[cs] * jnp.exp(b_last - bc)).astype(bf16)
        vc = v[cs]
        a = lax.dot_general(q_t, k_t, _NT, preferred_element_type=f32)
        a = jnp.where(seen, a, 0.0).astype(bf16)
        o = jnp.dot(a, vc, preferred_element_type=f32)
        o = o + lax.dot_general(q_t, state.astype(bf16), _NT, preferred_element_type=f32)
        outs[c] = o
        state = state * jnp.exp(b_last) + lax.dot_general(vc, k_end, _TN, preferred_element_type=f32)
    state_ref[...] = state
    acc_ref[rows, :] += jnp.concatenate(outs, axis=0)


def _hgrn_kernel(hq_ref, hff_ref, hfb_ref, hi_ref, hg_ref, lbf_ref, lbb_ref, ng_ref,
                 trif_ref, trib_ref, out_ref, acc_ref, stf_ref, stb_ref, *, seq):
    n_tiles = seq // HGRN_TILE
    acc_ref[...] = jnp.zeros_like(acc_ref)
    stf_ref[...] = jnp.zeros_like(stf_ref)
    stb_ref[...] = jnp.zeros_like(stb_ref)
    lbf = lbf_ref[...]
    lbb = lbb_ref[...]

    def tile(t, carry):
        fwd_rows = pl.ds(pl.multiple_of(t * HGRN_TILE, HGRN_TILE), HGRN_TILE)
        bwd_rows = pl.ds(pl.multiple_of((n_tiles - 1 - t) * HGRN_TILE, HGRN_TILE), HGRN_TILE)
        _hgrn_direction(fwd_rows, trif_ref, hq_ref, hff_ref, hi_ref, lbf, stf_ref, acc_ref,
                        reverse=False)
        _hgrn_direction(bwd_rows, trib_ref, hq_ref, hfb_ref, hi_ref, lbb, stb_ref, acc_ref,
                        reverse=True)
        return carry

    lax.fori_loop(0, n_tiles, tile, 0)

    ng = ng_ref[...]

    def finish(t, carry):
        rows = pl.ds(pl.multiple_of(t * HGRN_TILE, HGRN_TILE), HGRN_TILE)
        o = acc_ref[rows, :]
        hg = hg_ref[0, rows, :].astype(jnp.float32)
        o = _rms(o, ng) * (hg * jax.nn.sigmoid(hg))
        out_ref[0, rows, :] = o.astype(jnp.bfloat16)
        return carry

    lax.fori_loop(0, n_tiles, finish, 0)


def _chunk_triangles():
    r = jnp.arange(HGRN_TILE)[:, None]
    c = jnp.arange(HGRN_TILE)[None, :]
    same = (r // HGRN_CHUNK) == (c // HGRN_CHUNK)
    lower = (same & (r >= c)).astype(jnp.bfloat16)
    upper = (same & (r <= c)).astype(jnp.bfloat16)
    return lower, upper


def _hgrn(p3d, lb_f, lb_b, norm_g):
    b, s, _ = p3d.shape
    tri_f, tri_b = _chunk_triangles()

    def head_cols(col):
        return pl.BlockSpec((1, s, HGRN_DIM), lambda bi, h: (bi, 0, col // HGRN_DIM + h))

    vec = pl.BlockSpec((1, HGRN_DIM), lambda bi, h: (0, h))
    return pl.pallas_call(
        functools.partial(_hgrn_kernel, seq=s),
        grid=(b, HGRN_HEADS),
        in_specs=[head_cols(COL_HQ), head_cols(COL_HFF), head_cols(COL_HFB), head_cols(COL_HI),
                  head_cols(COL_HG), vec, vec, vec,
                  _resident(tri_f.shape), _resident(tri_b.shape)],
        out_specs=pl.BlockSpec((1, s, HGRN_DIM), lambda bi, h: (bi, 0, h)),
        out_shape=jax.ShapeDtypeStruct((b, s, HGRN_HEADS * HGRN_DIM), jnp.bfloat16),
        scratch_shapes=[
            pltpu.VMEM((s, HGRN_DIM), jnp.float32),
            pltpu.VMEM((HGRN_DIM, HGRN_DIM), jnp.float32),
            pltpu.VMEM((HGRN_DIM, HGRN_DIM), jnp.float32),
        ],
        compiler_params=pltpu.CompilerParams(
            dimension_semantics=("arbitrary", "arbitrary"), vmem_limit_bytes=VMEM_LIMIT),
        name="hgrn2",
    )(p3d, p3d, p3d, p3d, p3d, lb_f, lb_b, norm_g, tri_f, tri_b)


def _merge_kernel(x1_ref, gates_ref, o0_ref, o1_ref, o2_ref, l0_ref, l1_ref, l2_ref, yb_ref,
                  wa_ref, wb_ref, wo_ref, g2_ref, gf_ref, wg_ref, wu_ref, wd_ref,
                  y_ref, hs_ref, acc_ref):
    f32, bf16 = jnp.float32, jnp.bfloat16
    l0, l1, l2 = l0_ref[...], l1_ref[...], l2_ref[...]
    m = jnp.maximum(jnp.maximum(l0, l1), l2)
    e0, e1, e2 = jnp.exp(l0 - m), jnp.exp(l1 - m), jnp.exp(l2 - m)
    ya = (e0 * o0_ref[...] + e1 * o1_ref[...] + e2 * o2_ref[...]) / (e0 + e1 + e2)
    ma = jnp.dot(ya.astype(bf16), wa_ref[...], preferred_element_type=f32)
    mb = jnp.dot(yb_ref[...], wb_ref[...], preferred_element_type=f32)
    ga = gates_ref[:, :D_MODEL].astype(f32)
    gb = gates_ref[:, D_MODEL:].astype(f32)
    merged = jax.nn.sigmoid(ga) * ma + jax.nn.sigmoid(gb) * mb
    x2 = x1_ref[...] + jnp.dot(merged.astype(bf16), wo_ref[...], preferred_element_type=f32)
    hs_ref[...] = _rms(x2, g2_ref[...]).astype(bf16)
    _swiglu_into(acc_ref, hs_ref, wg_ref, wu_ref, wd_ref)
    x3 = x2 + 0.5 * acc_ref[...]
    y_ref[...] = _rms(x3, gf_ref[...])


def _merge(x1, p2d, attn, yb, wa, wb, wo, g2, gf, wg, wu, wd, tm):
    n = x1.shape[0]
    row = lambda i: (i, 0)
    wide = pl.BlockSpec((tm, D_MODEL), row)
    narrow = pl.BlockSpec((tm, ATTN_GROUP_WIDTH), row)
    (o0, l0), (o1, l1), (o2, l2) = attn
    return pl.pallas_call(
        _merge_kernel,
        grid=(n // tm,),
        in_specs=[wide, pl.BlockSpec((tm, 2 * D_MODEL), row)] + [narrow] * 6 + [wide] + [
            _resident(wa.shape), _resident(wb.shape), _resident(wo.shape),
            _resident((1, D_MODEL)), _resident((1, D_MODEL)),
            _resident(wg.shape), _resident(wu.shape), _resident(wd.shape)],
        out_specs=wide,
        out_shape=jax.ShapeDtypeStruct((n, D_MODEL), jnp.float32),
        scratch_shapes=[
            pltpu.VMEM((tm, D_MODEL), jnp.bfloat16),
            pltpu.VMEM((tm, D_MODEL), jnp.float32),
        ],
        compiler_params=pltpu.CompilerParams(
            dimension_semantics=("arbitrary",), vmem_limit_bytes=VMEM_LIMIT),
        name="merge_ffn2",
    )(x1, p2d, o0, o1, o2, l0, l1, l2, yb, wa, wb, wo, g2, gf, wg, wu, wd)


def _ffn_weights(w_gu, w_down):
    bf16 = jnp.bfloat16
    chunked = lambda w: w.reshape(D_MODEL, N_FF_CHUNKS, FF_CHUNK).transpose(1, 0, 2).astype(bf16)
    wg = chunked(w_gu[:, :D_FF])
    wu = chunked(w_gu[:, D_FF:])
    wd = w_down.reshape(N_FF_CHUNKS, FF_CHUNK, D_MODEL).astype(bf16)
    return wg, wu, wd


def _in_weights(w_in):
    aw = 3 * ATTN_GROUP_WIDTH
    a_q, a_k, a_v = w_in[:, :aw], w_in[:, aw:2 * aw], w_in[:, 2 * aw:3 * aw]
    rest = w_in[:, 3 * aw:]
    h_q, h_ff, h_fb, h_i, h_g, g_a, g_b = jnp.split(rest, 7, axis=1)
    scale = 1.0 / math.sqrt(ATTN_HEAD_DIM)

    def qkv(g):
        sl = slice(g * ATTN_GROUP_WIDTH, (g + 1) * ATTN_GROUP_WIDTH)
        return [a_q[:, sl] * scale, a_k[:, sl], a_v[:, sl]]

    cat = lambda cols: jnp.concatenate(cols, axis=1).astype(jnp.bfloat16)
    return cat([g_a, g_b, h_q, h_ff, h_fb, h_i, h_g] + qkv(0)), cat(qkv(1)), cat(qkv(2))


def _lower_bound(lb_raw, layer):
    return jnp.cumsum(jax.nn.softmax(lb_raw.astype(jnp.float32), axis=0), axis=0)[layer][None, :]


def _trunk(x, params):
    (g1, gm, ffn1_w, w_in, lb_f, lb_b, hgrn_norm, wa, wb, wo, g2, gf, ffn2_w) = params
    b, s, _ = x.shape
    n = b * s
    x1, h = _ffn1(x.reshape(n, D_MODEL), g1, gm, *ffn1_w, tm=512)
    w_main, w_dilated = w_in[0], w_in[1:]
    p2d = _proj(h, w_main, tm=256, name="proj_main")
    p3d = p2d.reshape(b, s, MAIN_WIDTH)
    o, lse = _attention(p3d, COL_ATTN // ATTN_GROUP_WIDTH, 1)
    attn = [(o.reshape(n, ATTN_GROUP_WIDTH), lse.reshape(n, ATTN_GROUP_WIDTH))]
    for d, w_qkv in zip(ATTN_DILATIONS[1:], w_dilated):
        sub_len = s // d
        h_sub = h.reshape(b, sub_len, d, D_MODEL).transpose(0, 2, 1, 3).reshape(n, D_MODEL)
        qkv = _proj(h_sub, w_qkv, tm=512, name=f"proj_d{d}").reshape(b * d, sub_len, ATTN_QKV_WIDTH)
        to_tokens = lambda t: t.reshape(b, d, sub_len, ATTN_GROUP_WIDTH).transpose(0, 2, 1, 3).reshape(
            n, ATTN_GROUP_WIDTH)
        o, lse = _attention(qkv, 0, d)
        attn.append((to_tokens(o), to_tokens(lse)))
    yb = _hgrn(p3d, lb_f, lb_b, hgrn_norm).reshape(n, D_MODEL)
    y = _merge(x1, p2d, attn, yb, wa, wb, wo, g2, gf, *ffn2_w, tm=256)
    return y.reshape(b, s, D_MODEL)


def kernel(x_prompt, x_sample, ffn1_norm, ffn1_w_gu, ffn1_w_down, mix_norm, w_in, hgrn_lb_fwd,
           hgrn_lb_bwd, hgrn_norm, w_branch_a, w_branch_b, w_out, ffn2_norm, ffn2_w_gu,
           ffn2_w_down, final_norm):
    bf16 = jnp.bfloat16
    layer = 0
    params = (
        ffn1_norm[layer][None, :], mix_norm[layer][None, :],
        _ffn_weights(ffn1_w_gu[layer], ffn1_w_down[layer]),
        _in_weights(w_in[layer]),
        _lower_bound(hgrn_lb_fwd, layer), _lower_bound(hgrn_lb_bwd, layer),
        hgrn_norm[layer][None, :],
        w_branch_a[layer].astype(bf16), w_branch_b[layer].astype(bf16), w_out[layer].astype(bf16),
        ffn2_norm[layer][None, :], final_norm[None, :],
        _ffn_weights(ffn2_w_gu[layer], ffn2_w_down[layer]),
    )
    return (_trunk(x_prompt, params), _trunk(x_sample, params))
```

```python
import functools
import math

import jax
import jax.numpy as jnp
from jax import lax
from jax.experimental import pallas as pl
from jax.experimental.pallas import tpu as pltpu

LANES = 128
D_MODEL = 1024
D_FF = 2816
RMS_EPS = 1e-6
MASK_BIAS = -1e30

ATTN_HEADS = 8
ATTN_HEAD_DIM = 64
ATTN_GROUP_WIDTH = ATTN_HEADS * ATTN_HEAD_DIM
HALF_TAPS = 64
ATTN_SUB = 128
ATTN_WIN = ATTN_SUB + 2 * HALF_TAPS

HGRN_HEADS = 8
HGRN_DIM = 128
HGRN_TILE = 128
HGRN_UNROLL = 4
_DONE = object()

FF_CHUNK = 256
N_FF_CHUNKS = D_FF // FF_CHUNK

FFN_ROWS = 512
PROJ_ROWS = 256
MIX_ROWS = 256

COL_GATES = 0
COL_HQ = 2048
COL_HFF = 3072
COL_HFB = 4096
COL_HI = 5120
COL_HG = 6144
COL_ATTN = 7168
ATTN_QKV_WIDTH = 3 * ATTN_GROUP_WIDTH
MAIN_WIDTH = COL_ATTN + ATTN_QKV_WIDTH
PROJ_CHUNK = 512

VMEM_LIMIT = 56 * 1024 * 1024

_NT = (((1,), (1,)), ((), ()))
_TN = (((0,), (0,)), ((), ()))


def _resident(shape):
    zeros = (0,) * len(shape)
    return pl.BlockSpec(shape, lambda *_: zeros, pipeline_mode=pl.Buffered(1))


def _rms(x, g):
    return x * lax.rsqrt(jnp.mean(x * x, axis=-1, keepdims=True) + RMS_EPS) * g


def _swiglu_into(acc_ref, h_ref, wg_ref, wu_ref, wd_ref):
    f32 = jnp.float32

    def gate_up(c):
        cols = slice(c * FF_CHUNK, (c + 1) * FF_CHUNK)
        a = jnp.dot(h_ref[...], wg_ref[:, cols], preferred_element_type=f32)
        b = jnp.dot(h_ref[...], wu_ref[:, cols], preferred_element_type=f32)
        return (a * jax.nn.sigmoid(a) * b).astype(jnp.bfloat16)

    def down(c, act):
        part = jnp.dot(act, wd_ref[c * FF_CHUNK:(c + 1) * FF_CHUNK, :], preferred_element_type=f32)
        if c == 0:
            acc_ref[...] = part
        else:
            acc_ref[...] += part

    act = gate_up(0)
    for c in range(1, N_FF_CHUNKS):
        nxt = gate_up(c)
        down(c - 1, act)
        act = nxt
    down(N_FF_CHUNKS - 1, act)


def _ffn_kernel(x_ref, gin_ref, wg_ref, wu_ref, wd_ref, y_ref, hs_ref, acc_ref):
    x = x_ref[...]
    hs_ref[...] = _rms(x, gin_ref[...]).astype(jnp.bfloat16)
    _swiglu_into(acc_ref, hs_ref, wg_ref, wu_ref, wd_ref)
    y_ref[...] = x + 0.5 * acc_ref[...]


def _ffn_final_kernel(x_ref, gin_ref, gout_ref, wg_ref, wu_ref, wd_ref, y_ref, hs_ref, acc_ref):
    x = x_ref[...]
    hs_ref[...] = _rms(x, gin_ref[...]).astype(jnp.bfloat16)
    _swiglu_into(acc_ref, hs_ref, wg_ref, wu_ref, wd_ref)
    y_ref[...] = _rms(x + 0.5 * acc_ref[...], gout_ref[...])


def _ffn(x2d, gin, weights, gout=None):
    n = x2d.shape[0]
    row = pl.BlockSpec((FFN_ROWS, D_MODEL), lambda i: (i, 0))
    gains = [gin] if gout is None else [gin, gout]
    return pl.pallas_call(
        _ffn_kernel if gout is None else _ffn_final_kernel,
        grid=(n // FFN_ROWS,),
        in_specs=[row] + [_resident((1, D_MODEL))] * len(gains) + [_resident(w.shape) for w in weights],
        out_specs=row,
        out_shape=jax.ShapeDtypeStruct((n, D_MODEL), jnp.float32),
        scratch_shapes=[
            pltpu.VMEM((FFN_ROWS, D_MODEL), jnp.bfloat16),
            pltpu.VMEM((FFN_ROWS, D_MODEL), jnp.float32),
        ],
        compiler_params=pltpu.CompilerParams(
            dimension_semantics=("arbitrary",), vmem_limit_bytes=VMEM_LIMIT),
        name="ffn" if gout is None else "ffn_final",
    )(x2d, *gains, *weights)


def _proj_kernel(x1_ref, gm_ref, wm_ref, w4_ref, w16_ref, pm_ref, p4_ref, p16_ref, hs_ref):
    f32, bf16 = jnp.float32, jnp.bfloat16
    hf = _rms(x1_ref[0], gm_ref[...])
    for c in range(D_MODEL // LANES):
        hs_ref[c] = hf[:, c * LANES:(c + 1) * LANES]
    h = hf.astype(bf16)
    for j in range(MAIN_WIDTH // PROJ_CHUNK):
        cols = slice(j * PROJ_CHUNK, (j + 1) * PROJ_CHUNK)
        pm_ref[0, :, cols] = jnp.dot(h, wm_ref[:, cols], preferred_element_type=f32).astype(bf16)
    for d, w_ref, p_ref in ((4, w4_ref, p4_ref), (16, w16_ref, p16_ref)):
        n = PROJ_ROWS // d
        hd = jnp.concatenate(
            [jnp.concatenate([hs_ref[c, pl.ds(r, n, stride=d), :]
                              for c in range(D_MODEL // LANES)], axis=1)
             for r in range(d)], axis=0).astype(bf16)
        for j in range(ATTN_QKV_WIDTH // PROJ_CHUNK):
            cols = slice(j * PROJ_CHUNK, (j + 1) * PROJ_CHUNK)
            res = jnp.dot(hd, w_ref[:, cols], preferred_element_type=f32).astype(bf16)
            for r in range(d):
                p_ref[0, r, :, cols] = res[r * n:(r + 1) * n]


def _proj(x1, gm, w_main, w_d4, w_d16):
    b, s, _ = x1.shape
    bf16 = jnp.bfloat16

    def dilated(d):
        spec = pl.BlockSpec((1, d, PROJ_ROWS // d, ATTN_QKV_WIDTH), lambda bi, i: (bi, 0, i, 0))
        return spec, jax.ShapeDtypeStruct((b, d, s // d, ATTN_QKV_WIDTH), bf16)

    (spec4, sds4), (spec16, sds16) = dilated(4), dilated(16)
    return pl.pallas_call(
        _proj_kernel,
        grid=(b, s // PROJ_ROWS),
        in_specs=[pl.BlockSpec((1, PROJ_ROWS, D_MODEL), lambda bi, i: (bi, i, 0)),
                  _resident((1, D_MODEL)),
                  _resident(w_main.shape), _resident(w_d4.shape), _resident(w_d16.shape)],
        out_specs=[pl.BlockSpec((1, PROJ_ROWS, MAIN_WIDTH), lambda bi, i: (bi, i, 0)), spec4, spec16],
        out_shape=[jax.ShapeDtypeStruct((b, s, MAIN_WIDTH), bf16), sds4, sds16],
        scratch_shapes=[pltpu.VMEM((D_MODEL // LANES, PROJ_ROWS, LANES), jnp.float32)],
        compiler_params=pltpu.CompilerParams(
            dimension_semantics=("arbitrary", "arbitrary"), vmem_limit_bytes=VMEM_LIMIT),
        name="proj",
    )(x1, gm, w_main, w_d4, w_d16)


def _attn_kernel(q_ref, kp_ref, kc_ref, kn_ref, vp_ref, vc_ref, vn_ref, o_ref, lse_ref,
                 *, dilation, sub_len, tq):
    q0 = pl.program_id(1) * tq
    kwin = jnp.concatenate([kp_ref[0], kc_ref[0], kn_ref[0]], axis=0)
    vwin = jnp.concatenate([vp_ref[0], vc_ref[0], vn_ref[0]], axis=0)

    row = lax.broadcasted_iota(jnp.int32, (ATTN_SUB, ATTN_WIN), 0)
    col = lax.broadcasted_iota(jnp.int32, (ATTN_SUB, ATTN_WIN), 1)
    rel = jnp.abs(col - HALF_TAPS - row)
    lane = lax.broadcasted_iota(jnp.int32, (ATTN_SUB, 2 * ATTN_HEAD_DIM), 1)
    first_head = lane < ATTN_HEAD_DIM
    ones = jnp.ones((ATTN_WIN, 2 * ATTN_HEAD_DIM), jnp.bfloat16)

    for sub in range(tq // ATTN_SUB):
        r0 = sub * ATTN_SUB
        kpos = q0 + (r0 - HALF_TAPS) + col
        valid = (rel <= HALF_TAPS) & (kpos >= 0) & (kpos < sub_len)
        base = jnp.where(valid, (-dilation) * rel.astype(jnp.float32), MASK_BIAS)
        for pair in range(ATTN_HEADS // 2):
            lanes = slice(pair * 128, (pair + 1) * 128)
            qp = q_ref[0, r0:r0 + ATTN_SUB, lanes]
            kw = kwin[r0:r0 + ATTN_WIN, lanes]
            vw = vwin[r0:r0 + ATTN_WIN, lanes]
            zero = jnp.zeros_like(qp)
            q2 = jnp.concatenate(
                [jnp.where(first_head, qp, zero), jnp.where(first_head, zero, qp)], axis=0)
            s = lax.dot_general(q2, kw, _NT, preferred_element_type=jnp.float32)
            halves = []
            for half in range(2):
                slope = 2.0 ** (-(2 * pair + half + 1))
                sh = s[half * ATTN_SUB:(half + 1) * ATTN_SUB] + slope * base
                m = jnp.max(sh, axis=-1, keepdims=True)
                halves.append((m, jnp.exp(sh - m).astype(jnp.bfloat16)))
            p2 = jnp.concatenate([halves[0][1], halves[1][1]], axis=0)
            vext = jnp.concatenate([vw, ones], axis=1)
            o2 = jnp.dot(p2, vext, preferred_element_type=jnp.float32)
            outs, lses = [], []
            for half in range(2):
                rows = slice(half * ATTN_SUB, (half + 1) * ATTN_SUB)
                num = o2[rows, :128]
                den = o2[rows, 128:]
                outs.append(num / den)
                lses.append(halves[half][0] + jnp.log(den))
            o_ref[0, r0:r0 + ATTN_SUB, lanes] = jnp.where(first_head, outs[0], outs[1])
            lse_ref[0, r0:r0 + ATTN_SUB, lanes] = jnp.where(first_head, lses[0], lses[1])


def _attention(qkv, col0, dilation):
    nsub, sub_len, _ = qkv.shape
    tq = 256 if sub_len >= 256 else ATTN_SUB
    halo_per_tile = tq // HALF_TAPS
    last_halo = sub_len // HALF_TAPS - 1

    def cur(which):
        return lambda si, i: (si, i, col0 + which)

    def prev(which):
        return lambda si, i: (si, jnp.maximum(i * halo_per_tile - 1, 0), col0 + which)

    def nxt(which):
        return lambda si, i: (si, jnp.minimum((i + 1) * halo_per_tile, last_halo), col0 + which)

    main = lambda which: pl.BlockSpec((1, tq, ATTN_GROUP_WIDTH), cur(which))
    halo = lambda fn, which: pl.BlockSpec((1, HALF_TAPS, ATTN_GROUP_WIDTH), fn(which))
    out_spec = pl.BlockSpec((1, tq, ATTN_GROUP_WIDTH), lambda si, i: (si, i, 0))
    out_sds = jax.ShapeDtypeStruct((nsub, sub_len, ATTN_GROUP_WIDTH), jnp.float32)
    return pl.pallas_call(
        functools.partial(_attn_kernel, dilation=dilation, sub_len=sub_len, tq=tq),
        grid=(nsub, sub_len // tq),
        in_specs=[main(0), halo(prev, 1), main(1), halo(nxt, 1), halo(prev, 2), main(2), halo(nxt, 2)],
        out_specs=[out_spec, out_spec],
        out_shape=[out_sds, out_sds],
        compiler_params=pltpu.CompilerParams(
            dimension_semantics=("arbitrary", "arbitrary"), vmem_limit_bytes=VMEM_LIMIT),
        name=f"attn_d{dilation}",
    )(qkv, qkv, qkv, qkv, qkv, qkv, qkv)


def _split3(x):
    hi = x.astype(jnp.bfloat16)
    r1 = x - hi.astype(jnp.float32)
    mid = r1.astype(jnp.bfloat16)
    lo = (r1 - mid.astype(jnp.float32)).astype(jnp.bfloat16)
    return hi, mid, lo


def _hgrn_tile_stages(rows, tri_ref, hq_ref, hf_ref, hi_ref, lb, carry, o_ref, *, reverse):
    f32, bf16 = jnp.float32, jnp.bfloat16
    z = hf_ref[0, rows, :].astype(f32)
    f = lb + (1.0 - lb) * jax.nn.sigmoid(z)
    k = 1.0 - f
    parts = jnp.dot(tri_ref[...], jnp.concatenate(_split3(jnp.log(f)), axis=1),
                    preferred_element_type=f32)
    yield
    bcum = (parts[:, 2 * HGRN_DIM:] + parts[:, HGRN_DIM:2 * HGRN_DIM]) + parts[:, :HGRN_DIM]
    mid, last = (HGRN_TILE // 2, 0) if reverse else (HGRN_TILE // 2 - 1, HGRN_TILE - 1)
    b_mid = bcum[mid:mid + 1]
    b_last = bcum[last:last + 1]
    hq = hq_ref[0, rows, :].astype(f32)
    q_t = (hq * jax.nn.sigmoid(hq) * jnp.exp(bcum - b_mid)).astype(bf16)
    k_mid = k * jnp.exp(b_mid - bcum)
    k_end = (k_mid * jnp.exp(b_last - b_mid)).astype(bf16)
    v = hi_ref[0, rows, :]
    a = lax.dot_general(q_t, k_mid.astype(bf16), _NT, preferred_element_type=f32)
    update = lax.dot_general(v, k_end, _TN, preferred_element_type=f32)
    yield
    state = carry["state"]
    inter = lax.dot_general(q_t, (state * jnp.exp(b_mid)).astype(bf16), _NT,
                            preferred_element_type=f32)
    carry["state"] = state * jnp.exp(b_last) + update
    yield
    tpos = lax.broadcasted_iota(jnp.int32, (HGRN_TILE, HGRN_TILE), 0)
    spos = lax.broadcasted_iota(jnp.int32, (HGRN_TILE, HGRN_TILE), 1)
    seen = (tpos <= spos) if reverse else (tpos >= spos)
    a = jnp.where(seen, a, 0.0).astype(bf16)
    o_ref[rows, :] = jnp.dot(a, v, preferred_element_type=f32) + inter


def _hgrn_kernel(hq_ref, hff_ref, hfb_ref, hi_ref, hg_ref, lbf_ref, lbb_ref, ng_ref,
                 trif_ref, trib_ref, out_ref, of_ref, ob_ref, stf_ref, stb_ref, *, seq):
    n_tiles = seq // HGRN_TILE
    stf_ref[...] = jnp.zeros_like(stf_ref)
    stb_ref[...] = jnp.zeros_like(stb_ref)
    lbf = lbf_ref[...]
    lbb = lbb_ref[...]

    def step(t, loop_carry):
        fwd = {"state": stf_ref[...]}
        bwd = {"state": stb_ref[...]}
        tiles = []
        for j in range(HGRN_UNROLL):
            tf = t * HGRN_UNROLL + j
            fwd_rows = pl.ds(pl.multiple_of(tf * HGRN_TILE, HGRN_TILE), HGRN_TILE)
            bwd_rows = pl.ds(pl.multiple_of((n_tiles - 1 - tf) * HGRN_TILE, HGRN_TILE), HGRN_TILE)
            tiles.append(_hgrn_tile_stages(fwd_rows, trif_ref, hq_ref, hff_ref, hi_ref, lbf, fwd,
                                           of_ref, reverse=False))
            tiles.append(_hgrn_tile_stages(bwd_rows, trib_ref, hq_ref, hfb_ref, hi_ref, lbb, bwd,
                                           ob_ref, reverse=True))
        while tiles:
            tiles = [g for g in tiles if next(g, _DONE) is not _DONE]
        stf_ref[...] = fwd["state"]
        stb_ref[...] = bwd["state"]
        return loop_carry

    lax.fori_loop(0, n_tiles // HGRN_UNROLL, step, 0)

    ng = ng_ref[...]

    def finish(t, carry):
        rows = pl.ds(pl.multiple_of(t * HGRN_TILE, HGRN_TILE), HGRN_TILE)
        o = of_ref[rows, :] + ob_ref[rows, :]
        hg = hg_ref[0, rows, :].astype(jnp.float32)
        o = _rms(o, ng) * (hg * jax.nn.sigmoid(hg))
        out_ref[0, rows, :] = o.astype(jnp.bfloat16)
        return carry

    lax.fori_loop(0, n_tiles, finish, 0)


def _tile_triangles():
    r = jnp.arange(HGRN_TILE)[:, None]
    c = jnp.arange(HGRN_TILE)[None, :]
    return (r >= c).astype(jnp.bfloat16), (r <= c).astype(jnp.bfloat16)


def _hgrn(p3d, lb_f, lb_b, norm_g):
    b, s, _ = p3d.shape
    tri_f, tri_b = _tile_triangles()

    def head_cols(col):
        return pl.BlockSpec((1, s, HGRN_DIM), lambda bi, h: (bi, 0, col // HGRN_DIM + h))

    vec = pl.BlockSpec((1, HGRN_DIM), lambda bi, h: (0, h))
    return pl.pallas_call(
        functools.partial(_hgrn_kernel, seq=s),
        grid=(b, HGRN_HEADS),
        in_specs=[head_cols(COL_HQ), head_cols(COL_HFF), head_cols(COL_HFB), head_cols(COL_HI),
                  head_cols(COL_HG), vec, vec, vec,
                  _resident(tri_f.shape), _resident(tri_b.shape)],
        out_specs=pl.BlockSpec((1, s, HGRN_DIM), lambda bi, h: (bi, 0, h)),
        out_shape=jax.ShapeDtypeStruct((b, s, HGRN_HEADS * HGRN_DIM), jnp.bfloat16),
        scratch_shapes=[
            pltpu.VMEM((s, HGRN_DIM), jnp.float32),
            pltpu.VMEM((s, HGRN_DIM), jnp.float32),
            pltpu.VMEM((HGRN_DIM, HGRN_DIM), jnp.float32),
            pltpu.VMEM((HGRN_DIM, HGRN_DIM), jnp.float32),
        ],
        compiler_params=pltpu.CompilerParams(
            dimension_semantics=("arbitrary", "arbitrary"), vmem_limit_bytes=VMEM_LIMIT),
        name="hgrn2",
    )(p3d, p3d, p3d, p3d, p3d, lb_f, lb_b, norm_g, tri_f, tri_b)


def _mix_kernel(x1_ref, gates_ref, o0_ref, l0_ref, o4_ref, l4_ref, o16_ref, l16_ref, yb_ref,
                wa_ref, wb_ref, wo_ref, x2_ref, so4_ref, sl4_ref, so16_ref, sl16_ref):
    f32, bf16 = jnp.float32, jnp.bfloat16
    for d, pairs in ((4, ((o4_ref, so4_ref), (l4_ref, sl4_ref))),
                     (16, ((o16_ref, so16_ref), (l16_ref, sl16_ref)))):
        n = MIX_ROWS // d
        for src_ref, dst_ref in pairs:
            for r in range(d):
                rows = src_ref[0, r]
                for c in range(ATTN_GROUP_WIDTH // LANES):
                    dst_ref[c, pl.ds(r, n, stride=d), :] = rows[:, c * LANES:(c + 1) * LANES]
    tokens = lambda ref: jnp.concatenate(
        [ref[c] for c in range(ATTN_GROUP_WIDTH // LANES)], axis=1)
    l0, l1, l2 = l0_ref[0], tokens(sl4_ref), tokens(sl16_ref)
    m = jnp.maximum(jnp.maximum(l0, l1), l2)
    e0, e1, e2 = jnp.exp(l0 - m), jnp.exp(l1 - m), jnp.exp(l2 - m)
    ya = (e0 * o0_ref[0] + e1 * tokens(so4_ref) + e2 * tokens(so16_ref)) / (e0 + e1 + e2)
    ma = jnp.dot(ya.astype(bf16), wa_ref[...], preferred_element_type=f32)
    mb = jnp.dot(yb_ref[0], wb_ref[...], preferred_element_type=f32)
    ga = gates_ref[0, :, :D_MODEL].astype(f32)
    gb = gates_ref[0, :, D_MODEL:].astype(f32)
    merged = jax.nn.sigmoid(ga) * ma + jax.nn.sigmoid(gb) * mb
    x2_ref[0] = x1_ref[0] + jnp.dot(merged.astype(bf16), wo_ref[...], preferred_element_type=f32)


def _mix(x1, pm, attn0, attn4, attn16, yb, wa, wb, wo):
    b, s, _ = x1.shape
    tok = lambda width: pl.BlockSpec((1, MIX_ROWS, width), lambda bi, i: (bi, i, 0))
    sub = lambda d: pl.BlockSpec((1, d, MIX_ROWS // d, ATTN_GROUP_WIDTH), lambda bi, i: (bi, 0, i, 0))
    scratch = pltpu.VMEM((ATTN_GROUP_WIDTH // LANES, MIX_ROWS, LANES), jnp.float32)
    return pl.pallas_call(
        _mix_kernel,
        grid=(b, s // MIX_ROWS),
        in_specs=[tok(D_MODEL), tok(2 * D_MODEL), tok(ATTN_GROUP_WIDTH), tok(ATTN_GROUP_WIDTH),
                  sub(4), sub(4), sub(16), sub(16), tok(D_MODEL),
                  _resident(wa.shape), _resident(wb.shape), _resident(wo.shape)],
        out_specs=tok(D_MODEL),
        out_shape=jax.ShapeDtypeStruct((b, s, D_MODEL), jnp.float32),
        scratch_shapes=[scratch] * 4,
        compiler_params=pltpu.CompilerParams(
            dimension_semantics=("arbitrary", "arbitrary"), vmem_limit_bytes=VMEM_LIMIT),
        name="mix",
    )(x1, pm, *attn0, *attn4, *attn16, yb, wa, wb, wo)


def _ffn_weights(w_gu, w_down):
    bf16 = jnp.bfloat16
    return w_gu[:, :D_FF].astype(bf16), w_gu[:, D_FF:].astype(bf16), w_down.astype(bf16)


def _in_weights(w_in):
    aw = 3 * ATTN_GROUP_WIDTH
    a_q, a_k, a_v = w_in[:, :aw], w_in[:, aw:2 * aw], w_in[:, 2 * aw:3 * aw]
    rest = w_in[:, 3 * aw:]
    h_q, h_ff, h_fb, h_i, h_g, g_a, g_b = jnp.split(rest, 7, axis=1)
    scale = 1.0 / math.sqrt(ATTN_HEAD_DIM)

    def qkv(g):
        sl = slice(g * ATTN_GROUP_WIDTH, (g + 1) * ATTN_GROUP_WIDTH)
        return [a_q[:, sl] * scale, a_k[:, sl], a_v[:, sl]]

    cat = lambda cols: jnp.concatenate(cols, axis=1).astype(jnp.bfloat16)
    return cat([g_a, g_b, h_q, h_ff, h_fb, h_i, h_g] + qkv(0)), cat(qkv(1)), cat(qkv(2))


def _lower_bound(lb_raw, layer):
    return jnp.cumsum(jax.nn.softmax(lb_raw.astype(jnp.float32), axis=0), axis=0)[layer][None, :]


def _trunk(x, params):
    (g1, gm, ffn1_w, w_in, lb_f, lb_b, hgrn_norm, wa, wb, wo, g2, gf, ffn2_w) = params
    b, s, _ = x.shape
    n = b * s
    x1 = _ffn(x.reshape(n, D_MODEL), g1, ffn1_w).reshape(b, s, D_MODEL)
    pm, p4, p16 = _proj(x1, gm, *w_in)
    attn0 = _attention(pm, COL_ATTN // ATTN_GROUP_WIDTH, 1)
    dilated = []
    for d, qkv in ((4, p4), (16, p16)):
        o, lse = _attention(qkv.reshape(b * d, s // d, ATTN_QKV_WIDTH), 0, d)
        dilated.append((o.reshape(b, d, s // d, ATTN_GROUP_WIDTH),
                        lse.reshape(b, d, s // d, ATTN_GROUP_WIDTH)))
    yb = _hgrn(pm, lb_f, lb_b, hgrn_norm)
    x2 = _mix(x1, pm, attn0, *dilated, yb, wa, wb, wo)
    y = _ffn(x2.reshape(n, D_MODEL), g2, ffn2_w, gout=gf)
    return y.reshape(b, s, D_MODEL)


def kernel(x_prompt, x_sample, ffn1_norm, ffn1_w_gu, ffn1_w_down, mix_norm, w_in, hgrn_lb_fwd,
           hgrn_lb_bwd, hgrn_norm, w_branch_a, w_branch_b, w_out, ffn2_norm, ffn2_w_gu,
           ffn2_w_down, final_norm):
    bf16 = jnp.bfloat16
    layer = 0
    params = (
        ffn1_norm[layer][None, :], mix_norm[layer][None, :],
        _ffn_weights(ffn1_w_gu[layer], ffn1_w_down[layer]),
        _in_weights(w_in[layer]),
        _lower_bound(hgrn_lb_fwd, layer), _lower_bound(hgrn_lb_bwd, layer),
        hgrn_norm[layer][None, :],
        w_branch_a[layer].astype(bf16), w_branch_b[layer].astype(bf16), w_out[layer].astype(bf16),
        ffn2_norm[layer][None, :], final_norm[None, :],
        _ffn_weights(ffn2_w_gu[layer], ffn2_w_down[layer]),
    )
    return (_trunk(x_prompt, params), _trunk(x_sample, params))
```

```python
import functools
import math

import jax
import jax.numpy as jnp
from jax import lax
from jax.experimental import pallas as pl
from jax.experimental.pallas import tpu as pltpu

LANES = 128
D_MODEL = 1024
D_FF = 2816
RMS_EPS = 1e-6
MASK_BIAS = -1e30

ATTN_HEADS = 8
ATTN_HEAD_DIM = 64
ATTN_GROUP_WIDTH = ATTN_HEADS * ATTN_HEAD_DIM
HALF_TAPS = 64
ATTN_SUB = 128
ATTN_WIN = ATTN_SUB + 2 * HALF_TAPS

HGRN_HEADS = 8
HGRN_DIM = 128
HGRN_TILE = 128
HGRN_UNROLL = 4
HGRN_FINISH_ROWS = 512
_DONE = object()

FF_CHUNK = 256
N_FF_CHUNKS = D_FF // FF_CHUNK

FFN_ROWS = 512
PROJ_ROWS = 256
MIX_ROWS = 256

COL_GATES = 0
COL_QS = 2048
COL_FWD = 3072
COL_BWD = 6144
COL_HI = 9216
COL_HG = 10240
COL_ATTN = 11264
ATTN_QKV_WIDTH = 3 * ATTN_GROUP_WIDTH
MAIN_WIDTH = COL_ATTN + ATTN_QKV_WIDTH
WCOL_GATES, WCOL_HQ, WCOL_HFF, WCOL_HFB, WCOL_HI, WCOL_HG, WCOL_ATTN = (
    0, 2048, 3072, 4096, 5120, 6144, 7168)
PROJ_CHUNK = 512

VMEM_LIMIT = 56 * 1024 * 1024

_NT = (((1,), (1,)), ((), ()))
_TN = (((0,), (0,)), ((), ()))


def _resident(shape):
    zeros = (0,) * len(shape)
    return pl.BlockSpec(shape, lambda *_: zeros, pipeline_mode=pl.Buffered(1))


def _rms(x, g):
    return x * lax.rsqrt(jnp.mean(x * x, axis=-1, keepdims=True) + RMS_EPS) * g


def _swiglu_into(acc_ref, h_ref, wg_ref, wu_ref, wd_ref):
    f32 = jnp.float32

    def gate_up(c):
        cols = slice(c * FF_CHUNK, (c + 1) * FF_CHUNK)
        a = jnp.dot(h_ref[...], wg_ref[:, cols], preferred_element_type=f32)
        b = jnp.dot(h_ref[...], wu_ref[:, cols], preferred_element_type=f32)
        return (a * jax.nn.sigmoid(a) * b).astype(jnp.bfloat16)

    def down(c, act):
        part = jnp.dot(act, wd_ref[c * FF_CHUNK:(c + 1) * FF_CHUNK, :], preferred_element_type=f32)
        if c == 0:
            acc_ref[...] = part
        else:
            acc_ref[...] += part

    act = gate_up(0)
    for c in range(1, N_FF_CHUNKS):
        nxt = gate_up(c)
        down(c - 1, act)
        act = nxt
    down(N_FF_CHUNKS - 1, act)


def _ffn_kernel(x_ref, gin_ref, wg_ref, wu_ref, wd_ref, y_ref, hs_ref, acc_ref):
    x = x_ref[...]
    hs_ref[...] = _rms(x, gin_ref[...]).astype(jnp.bfloat16)
    _swiglu_into(acc_ref, hs_ref, wg_ref, wu_ref, wd_ref)
    y_ref[...] = x + 0.5 * acc_ref[...]


def _ffn_final_kernel(x_ref, gin_ref, gout_ref, wg_ref, wu_ref, wd_ref, y_ref, hs_ref, acc_ref):
    x = x_ref[...]
    hs_ref[...] = _rms(x, gin_ref[...]).astype(jnp.bfloat16)
    _swiglu_into(acc_ref, hs_ref, wg_ref, wu_ref, wd_ref)
    y_ref[...] = _rms(x + 0.5 * acc_ref[...], gout_ref[...])


def _ffn(x2d, gin, weights, gout=None):
    n = x2d.shape[0]
    row = pl.BlockSpec((FFN_ROWS, D_MODEL), lambda i: (i, 0))
    gains = [gin] if gout is None else [gin, gout]
    return pl.pallas_call(
        _ffn_kernel if gout is None else _ffn_final_kernel,
        grid=(n // FFN_ROWS,),
        in_specs=[row] + [_resident((1, D_MODEL))] * len(gains) + [_resident(w.shape) for w in weights],
        out_specs=row,
        out_shape=jax.ShapeDtypeStruct((n, D_MODEL), jnp.float32),
        scratch_shapes=[
            pltpu.VMEM((FFN_ROWS, D_MODEL), jnp.bfloat16),
            pltpu.VMEM((FFN_ROWS, D_MODEL), jnp.float32),
        ],
        compiler_params=pltpu.CompilerParams(
            dimension_semantics=("arbitrary",), vmem_limit_bytes=VMEM_LIMIT),
        name="ffn" if gout is None else "ffn_final",
    )(x2d, *gains, *weights)


def _proj_kernel(x1_ref, gm_ref, lbf_ref, lbb_ref, wm_ref, w4_ref, w16_ref,
                 pm_ref, p4_ref, p16_ref, hs_ref):
    f32, bf16 = jnp.float32, jnp.bfloat16
    hf = _rms(x1_ref[0], gm_ref[...])
    for c in range(D_MODEL // LANES):
        hs_ref[c] = hf[:, c * LANES:(c + 1) * LANES]
    h = hf.astype(bf16)

    def project(wcol):
        return jnp.dot(h, wm_ref[:, wcol:wcol + PROJ_CHUNK], preferred_element_type=f32)

    def put(col, val):
        pm_ref[0, :, col:col + PROJ_CHUNK] = val.astype(bf16)

    def plain(wcol0, col0, width):
        for off in range(0, width, PROJ_CHUNK):
            put(col0 + off, project(wcol0 + off))

    plain(WCOL_GATES, COL_GATES, 2 * D_MODEL)
    for off in range(0, D_MODEL, PROJ_CHUNK):
        hq = project(WCOL_HQ + off)
        put(COL_QS + off, hq * jax.nn.sigmoid(hq))
    for wcol0, col0, lb_ref in ((WCOL_HFF, COL_FWD, lbf_ref), (WCOL_HFB, COL_BWD, lbb_ref)):
        for off in range(0, D_MODEL, PROJ_CHUNK):
            lb = lb_ref[:, off:off + PROJ_CHUNK]
            f = lb + (1.0 - lb) * jax.nn.sigmoid(project(wcol0 + off))
            log_f = jnp.log(f)
            hi = log_f.astype(bf16)
            put(col0 + off, hi)
            put(col0 + D_MODEL + off, log_f - hi.astype(f32))
            put(col0 + 2 * D_MODEL + off, 1.0 - f)
    plain(WCOL_HI, COL_HI, D_MODEL)
    plain(WCOL_HG, COL_HG, D_MODEL)
    plain(WCOL_ATTN, COL_ATTN, ATTN_QKV_WIDTH)
    for d, w_ref, p_ref in ((4, w4_ref, p4_ref), (16, w16_ref, p16_ref)):
        n = PROJ_ROWS // d
        hd = jnp.concatenate(
            [jnp.concatenate([hs_ref[c, pl.ds(r, n, stride=d), :]
                              for c in range(D_MODEL // LANES)], axis=1)
             for r in range(d)], axis=0).astype(bf16)
        for j in range(ATTN_QKV_WIDTH // PROJ_CHUNK):
            cols = slice(j * PROJ_CHUNK, (j + 1) * PROJ_CHUNK)
            res = jnp.dot(hd, w_ref[:, cols], preferred_element_type=f32).astype(bf16)
            for r in range(d):
                p_ref[0, r, :, cols] = res[r * n:(r + 1) * n]


def _proj(x1, gm, lb_f, lb_b, w_main, w_d4, w_d16):
    b, s, _ = x1.shape
    bf16 = jnp.bfloat16

    def dilated(d):
        spec = pl.BlockSpec((1, d, PROJ_ROWS // d, ATTN_QKV_WIDTH), lambda bi, i: (bi, 0, i, 0))
        return spec, jax.ShapeDtypeStruct((b, d, s // d, ATTN_QKV_WIDTH), bf16)

    (spec4, sds4), (spec16, sds16) = dilated(4), dilated(16)
    return pl.pallas_call(
        _proj_kernel,
        grid=(b, s // PROJ_ROWS),
        in_specs=[pl.BlockSpec((1, PROJ_ROWS, D_MODEL), lambda bi, i: (bi, i, 0)),
                  _resident((1, D_MODEL)), _resident((1, D_MODEL)), _resident((1, D_MODEL)),
                  _resident(w_main.shape), _resident(w_d4.shape), _resident(w_d16.shape)],
        out_specs=[pl.BlockSpec((1, PROJ_ROWS, MAIN_WIDTH), lambda bi, i: (bi, i, 0)), spec4, spec16],
        out_shape=[jax.ShapeDtypeStruct((b, s, MAIN_WIDTH), bf16), sds4, sds16],
        scratch_shapes=[pltpu.VMEM((D_MODEL // LANES, PROJ_ROWS, LANES), jnp.float32)],
        compiler_params=pltpu.CompilerParams(
            dimension_semantics=("arbitrary", "arbitrary"), vmem_limit_bytes=VMEM_LIMIT),
        name="proj",
    )(x1, gm, lb_f, lb_b, w_main, w_d4, w_d16)


def _attn_kernel(q_ref, kp_ref, kc_ref, kn_ref, vp_ref, vc_ref, vn_ref, o_ref, lse_ref,
                 *, dilation, sub_len, tq):
    q0 = pl.program_id(1) * tq
    kwin = jnp.concatenate([kp_ref[0], kc_ref[0], kn_ref[0]], axis=0)
    vwin = jnp.concatenate([vp_ref[0], vc_ref[0], vn_ref[0]], axis=0)

    row = lax.broadcasted_iota(jnp.int32, (ATTN_SUB, ATTN_WIN), 0)
    col = lax.broadcasted_iota(jnp.int32, (ATTN_SUB, ATTN_WIN), 1)
    rel = jnp.abs(col - HALF_TAPS - row)
    lane = lax.broadcasted_iota(jnp.int32, (ATTN_SUB, 2 * ATTN_HEAD_DIM), 1)
    first_head = lane < ATTN_HEAD_DIM
    ones = jnp.ones((ATTN_WIN, 2 * ATTN_HEAD_DIM), jnp.bfloat16)

    for sub in range(tq // ATTN_SUB):
        r0 = sub * ATTN_SUB
        kpos = q0 + (r0 - HALF_TAPS) + col
        valid = (rel <= HALF_TAPS) & (kpos >= 0) & (kpos < sub_len)
        base = jnp.where(valid, (-dilation) * rel.astype(jnp.float32), MASK_BIAS)
        for pair in range(ATTN_HEADS // 2):
            lanes = slice(pair * 128, (pair + 1) * 128)
            qp = q_ref[0, r0:r0 + ATTN_SUB, lanes]
            kw = kwin[r0:r0 + ATTN_WIN, lanes]
            vw = vwin[r0:r0 + ATTN_WIN, lanes]
            zero = jnp.zeros_like(qp)
            q2 = jnp.concatenate(
                [jnp.where(first_head, qp, zero), jnp.where(first_head, zero, qp)], axis=0)
            s = lax.dot_general(q2, kw, _NT, preferred_element_type=jnp.float32)
            halves = []
            for half in range(2):
                slope = 2.0 ** (-(2 * pair + half + 1))
                sh = s[half * ATTN_SUB:(half + 1) * ATTN_SUB] + slope * base
                m = jnp.max(sh, axis=-1, keepdims=True)
                halves.append((m, jnp.exp(sh - m).astype(jnp.bfloat16)))
            p2 = jnp.concatenate([halves[0][1], halves[1][1]], axis=0)
            vext = jnp.concatenate([vw, ones], axis=1)
            o2 = jnp.dot(p2, vext, preferred_element_type=jnp.float32)
            outs, lses = [], []
            for half in range(2):
                rows = slice(half * ATTN_SUB, (half + 1) * ATTN_SUB)
                num = o2[rows, :128]
                den = o2[rows, 128:]
                outs.append(num / den)
                lses.append(halves[half][0] + jnp.log(den))
            o_ref[0, r0:r0 + ATTN_SUB, lanes] = jnp.where(first_head, outs[0], outs[1])
            lse_ref[0, r0:r0 + ATTN_SUB, lanes] = jnp.where(first_head, lses[0], lses[1])


def _attention(qkv, col0, dilation):
    nsub, sub_len, _ = qkv.shape
    tq = next(t for t in (512, 256, ATTN_SUB) if sub_len % t == 0)
    halo_per_tile = tq // HALF_TAPS
    last_halo = sub_len // HALF_TAPS - 1

    def cur(which):
        return lambda si, i: (si, i, col0 + which)

    def prev(which):
        return lambda si, i: (si, jnp.maximum(i * halo_per_tile - 1, 0), col0 + which)

    def nxt(which):
        return lambda si, i: (si, jnp.minimum((i + 1) * halo_per_tile, last_halo), col0 + which)

    main = lambda which: pl.BlockSpec((1, tq, ATTN_GROUP_WIDTH), cur(which))
    halo = lambda fn, which: pl.BlockSpec((1, HALF_TAPS, ATTN_GROUP_WIDTH), fn(which))
    out_spec = pl.BlockSpec((1, tq, ATTN_GROUP_WIDTH), lambda si, i: (si, i, 0))
    out_sds = jax.ShapeDtypeStruct((nsub, sub_len, ATTN_GROUP_WIDTH), jnp.float32)
    return pl.pallas_call(
        functools.partial(_attn_kernel, dilation=dilation, sub_len=sub_len, tq=tq),
        grid=(nsub, sub_len // tq),
        in_specs=[main(0), halo(prev, 1), main(1), halo(nxt, 1), halo(prev, 2), main(2), halo(nxt, 2)],
        out_specs=[out_spec, out_spec],
        out_shape=[out_sds, out_sds],
        compiler_params=pltpu.CompilerParams(
            dimension_semantics=("arbitrary", "arbitrary"), vmem_limit_bytes=VMEM_LIMIT),
        name=f"attn_d{dilation}",
    )(qkv, qkv, qkv, qkv, qkv, qkv, qkv)


def _hgrn_tile_stages(rows, tri_ref, qs_ref, gate, v_ref, carry, o_ref, *, reverse):
    f32, bf16 = jnp.float32, jnp.bfloat16
    lf_hi_ref, lf_mid_ref, k_ref = gate
    parts = jnp.dot(tri_ref[...],
                    jnp.concatenate([lf_hi_ref[0, rows, :], lf_mid_ref[0, rows, :]], axis=1),
                    preferred_element_type=f32)
    yield
    bcum = parts[:, HGRN_DIM:] + parts[:, :HGRN_DIM]
    mid, last = (HGRN_TILE // 2, 0) if reverse else (HGRN_TILE // 2 - 1, HGRN_TILE - 1)
    b_mid = bcum[mid:mid + 1]
    b_last = bcum[last:last + 1]
    q_t = (qs_ref[0, rows, :].astype(f32) * jnp.exp(bcum - b_mid)).astype(bf16)
    k_mid = k_ref[0, rows, :].astype(f32) * jnp.exp(b_mid - bcum)
    k_end = (k_mid * jnp.exp(b_last - b_mid)).astype(bf16)
    v = v_ref[0, rows, :]
    a = lax.dot_general(q_t, k_mid.astype(bf16), _NT, preferred_element_type=f32)
    update = lax.dot_general(v, k_end, _TN, preferred_element_type=f32)
    yield
    state = carry["state"]
    inter = lax.dot_general(q_t, (state * jnp.exp(b_mid)).astype(bf16), _NT,
                            preferred_element_type=f32)
    carry["state"] = state * jnp.exp(b_last) + update
    yield
    tpos = lax.broadcasted_iota(jnp.int32, (HGRN_TILE, HGRN_TILE), 0)
    spos = lax.broadcasted_iota(jnp.int32, (HGRN_TILE, HGRN_TILE), 1)
    seen = (tpos <= spos) if reverse else (tpos >= spos)
    a = jnp.where(seen, a, 0.0).astype(bf16)
    o_ref[rows, :] = jnp.dot(a, v, preferred_element_type=f32) + inter


def _hgrn_kernel(qs_ref, fhi_ref, fmid_ref, fk_ref, bhi_ref, bmid_ref, bk_ref, v_ref, hg_ref,
                 ng_ref, trif_ref, trib_ref, out_ref, of_ref, ob_ref, stf_ref, stb_ref, *, seq):
    n_tiles = seq // HGRN_TILE
    stf_ref[...] = jnp.zeros_like(stf_ref)
    stb_ref[...] = jnp.zeros_like(stb_ref)
    fwd_gate = (fhi_ref, fmid_ref, fk_ref)
    bwd_gate = (bhi_ref, bmid_ref, bk_ref)

    def step(t, loop_carry):
        fwd = {"state": stf_ref[...]}
        bwd = {"state": stb_ref[...]}
        tiles = []
        for j in range(HGRN_UNROLL):
            tf = t * HGRN_UNROLL + j
            fwd_rows = pl.ds(pl.multiple_of(tf * HGRN_TILE, HGRN_TILE), HGRN_TILE)
            bwd_rows = pl.ds(pl.multiple_of((n_tiles - 1 - tf) * HGRN_TILE, HGRN_TILE), HGRN_TILE)
            tiles.append(_hgrn_tile_stages(fwd_rows, trif_ref, qs_ref, fwd_gate, v_ref, fwd,
                                           of_ref, reverse=False))
            tiles.append(_hgrn_tile_stages(bwd_rows, trib_ref, qs_ref, bwd_gate, v_ref, bwd,
                                           ob_ref, reverse=True))
        while tiles:
            tiles = [g for g in tiles if next(g, _DONE) is not _DONE]
        stf_ref[...] = fwd["state"]
        stb_ref[...] = bwd["state"]
        return loop_carry

    lax.fori_loop(0, n_tiles // HGRN_UNROLL, step, 0)

    ng = ng_ref[...]

    def finish(t, carry):
        rows = pl.ds(pl.multiple_of(t * HGRN_FINISH_ROWS, HGRN_FINISH_ROWS), HGRN_FINISH_ROWS)
        o = of_ref[rows, :] + ob_ref[rows, :]
        hg = hg_ref[0, rows, :].astype(jnp.float32)
        o = _rms(o, ng) * (hg * jax.nn.sigmoid(hg))
        out_ref[0, rows, :] = o.astype(jnp.bfloat16)
        return carry

    lax.fori_loop(0, seq // HGRN_FINISH_ROWS, finish, 0)


def _tile_triangles():
    r = jnp.arange(HGRN_TILE)[:, None]
    c = jnp.arange(HGRN_TILE)[None, :]
    return (r >= c).astype(jnp.bfloat16), (r <= c).astype(jnp.bfloat16)


def _hgrn(p3d, norm_g):
    b, s, _ = p3d.shape
    tri_f, tri_b = _tile_triangles()

    def head_cols(col):
        return pl.BlockSpec((1, s, HGRN_DIM), lambda bi, h: (bi, 0, col // HGRN_DIM + h))

    planes = [COL_QS] + [c0 + i * D_MODEL for c0 in (COL_FWD, COL_BWD) for i in range(3)] + [
        COL_HI, COL_HG]
    return pl.pallas_call(
        functools.partial(_hgrn_kernel, seq=s),
        grid=(b, HGRN_HEADS),
        in_specs=[head_cols(col) for col in planes] + [
            pl.BlockSpec((1, HGRN_DIM), lambda bi, h: (0, h)),
            _resident(tri_f.shape), _resident(tri_b.shape)],
        out_specs=pl.BlockSpec((1, s, HGRN_DIM), lambda bi, h: (bi, 0, h)),
        out_shape=jax.ShapeDtypeStruct((b, s, HGRN_HEADS * HGRN_DIM), jnp.bfloat16),
        scratch_shapes=[
            pltpu.VMEM((s, HGRN_DIM), jnp.float32),
            pltpu.VMEM((s, HGRN_DIM), jnp.float32),
            pltpu.VMEM((HGRN_DIM, HGRN_DIM), jnp.float32),
            pltpu.VMEM((HGRN_DIM, HGRN_DIM), jnp.float32),
        ],
        compiler_params=pltpu.CompilerParams(
            dimension_semantics=("arbitrary", "arbitrary"), vmem_limit_bytes=VMEM_LIMIT),
        name="hgrn2",
    )(*([p3d] * len(planes)), norm_g, tri_f, tri_b)


def _mix_kernel(x1_ref, gates_ref, o0_ref, l0_ref, o4_ref, l4_ref, o16_ref, l16_ref, yb_ref,
                wa_ref, wb_ref, wo_ref, x2_ref, so4_ref, sl4_ref, so16_ref, sl16_ref):
    f32, bf16 = jnp.float32, jnp.bfloat16
    for d, pairs in ((4, ((o4_ref, so4_ref), (l4_ref, sl4_ref))),
                     (16, ((o16_ref, so16_ref), (l16_ref, sl16_ref)))):
        n = MIX_ROWS // d
        for src_ref, dst_ref in pairs:
            for r in range(d):
                rows = src_ref[0, r]
                for c in range(ATTN_GROUP_WIDTH // LANES):
                    dst_ref[c, pl.ds(r, n, stride=d), :] = rows[:, c * LANES:(c + 1) * LANES]
    tokens = lambda ref: jnp.concatenate(
        [ref[c] for c in range(ATTN_GROUP_WIDTH // LANES)], axis=1)
    l0, l1, l2 = l0_ref[0], tokens(sl4_ref), tokens(sl16_ref)
    m = jnp.maximum(jnp.maximum(l0, l1), l2)
    e0, e1, e2 = jnp.exp(l0 - m), jnp.exp(l1 - m), jnp.exp(l2 - m)
    ya = (e0 * o0_ref[0] + e1 * tokens(so4_ref) + e2 * tokens(so16_ref)) / (e0 + e1 + e2)
    ma = jnp.dot(ya.astype(bf16), wa_ref[...], preferred_element_type=f32)
    mb = jnp.dot(yb_ref[0], wb_ref[...], preferred_element_type=f32)
    ga = gates_ref[0, :, :D_MODEL].astype(f32)
    gb = gates_ref[0, :, D_MODEL:].astype(f32)
    merged = jax.nn.sigmoid(ga) * ma + jax.nn.sigmoid(gb) * mb
    x2_ref[0] = x1_ref[0] + jnp.dot(merged.astype(bf16), wo_ref[...], preferred_element_type=f32)


def _mix(x1, pm, attn0, attn4, attn16, yb, wa, wb, wo):
    b, s, _ = x1.shape
    tok = lambda width: pl.BlockSpec((1, MIX_ROWS, width), lambda bi, i: (bi, i, 0))
    sub = lambda d: pl.BlockSpec((1, d, MIX_ROWS // d, ATTN_GROUP_WIDTH), lambda bi, i: (bi, 0, i, 0))
    scratch = pltpu.VMEM((ATTN_GROUP_WIDTH // LANES, MIX_ROWS, LANES), jnp.float32)
    return pl.pallas_call(
        _mix_kernel,
        grid=(b, s // MIX_ROWS),
        in_specs=[tok(D_MODEL), tok(2 * D_MODEL), tok(ATTN_GROUP_WIDTH), tok(ATTN_GROUP_WIDTH),
                  sub(4), sub(4), sub(16), sub(16), tok(D_MODEL),
                  _resident(wa.shape), _resident(wb.shape), _resident(wo.shape)],
        out_specs=tok(D_MODEL),
        out_shape=jax.ShapeDtypeStruct((b, s, D_MODEL), jnp.float32),
        scratch_shapes=[scratch] * 4,
        compiler_params=pltpu.CompilerParams(
            dimension_semantics=("arbitrary", "arbitrary"), vmem_limit_bytes=VMEM_LIMIT),
        name="mix",
    )(x1, pm, *attn0, *attn4, *attn16, yb, wa, wb, wo)


def _ffn_weights(w_gu, w_down):
    bf16 = jnp.bfloat16
    return w_gu[:, :D_FF].astype(bf16), w_gu[:, D_FF:].astype(bf16), w_down.astype(bf16)


def _in_weights(w_in):
    aw = 3 * ATTN_GROUP_WIDTH
    a_q, a_k, a_v = w_in[:, :aw], w_in[:, aw:2 * aw], w_in[:, 2 * aw:3 * aw]
    rest = w_in[:, 3 * aw:]
    h_q, h_ff, h_fb, h_i, h_g, g_a, g_b = jnp.split(rest, 7, axis=1)
    scale = 1.0 / math.sqrt(ATTN_HEAD_DIM)

    def qkv(g):
        sl = slice(g * ATTN_GROUP_WIDTH, (g + 1) * ATTN_GROUP_WIDTH)
        return [a_q[:, sl] * scale, a_k[:, sl], a_v[:, sl]]

    cat = lambda cols: jnp.concatenate(cols, axis=1).astype(jnp.bfloat16)
    return cat([g_a, g_b, h_q, h_ff, h_fb, h_i, h_g] + qkv(0)), cat(qkv(1)), cat(qkv(2))


def _lower_bound(lb_raw, layer):
    return jnp.cumsum(jax.nn.softmax(lb_raw.astype(jnp.float32), axis=0), axis=0)[layer][None, :]


def _trunk(x, params):
    (g1, gm, ffn1_w, w_in, lb_f, lb_b, hgrn_norm, wa, wb, wo, g2, gf, ffn2_w) = params
    b, s, _ = x.shape
    n = b * s
    x1 = _ffn(x.reshape(n, D_MODEL), g1, ffn1_w).reshape(b, s, D_MODEL)
    pm, p4, p16 = _proj(x1, gm, lb_f, lb_b, *w_in)
    attn0 = _attention(pm, COL_ATTN // ATTN_GROUP_WIDTH, 1)
    dilated = []
    for d, qkv in ((4, p4), (16, p16)):
        o, lse = _attention(qkv.reshape(b * d, s // d, ATTN_QKV_WIDTH), 0, d)
        dilated.append((o.reshape(b, d, s // d, ATTN_GROUP_WIDTH),
                        lse.reshape(b, d, s // d, ATTN_GROUP_WIDTH)))
    yb = _hgrn(pm, hgrn_norm)
    x2 = _mix(x1, pm, attn0, *dilated, yb, wa, wb, wo)
    y = _ffn(x2.reshape(n, D_MODEL), g2, ffn2_w, gout=gf)
    return y.reshape(b, s, D_MODEL)


def kernel(x_prompt, x_sample, ffn1_norm, ffn1_w_gu, ffn1_w_down, mix_norm, w_in, hgrn_lb_fwd,
           hgrn_lb_bwd, hgrn_norm, w_branch_a, w_branch_b, w_out, ffn2_norm, ffn2_w_gu,
           ffn2_w_down, final_norm):
    bf16 = jnp.bfloat16
    layer = 0
    params = (
        ffn1_norm[layer][None, :], mix_norm[layer][None, :],
        _ffn_weights(ffn1_w_gu[layer], ffn1_w_down[layer]),
        _in_weights(w_in[layer]),
        _lower_bound(hgrn_lb_fwd, layer), _lower_bound(hgrn_lb_bwd, layer),
        hgrn_norm[layer][None, :],
        w_branch_a[layer].astype(bf16), w_branch_b[layer].astype(bf16), w_out[layer].astype(bf16),
        ffn2_norm[layer][None, :], final_norm[None, :],
        _ffn_weights(ffn2_w_gu[layer], ffn2_w_down[layer]),
    )
    return (_trunk(x_prompt, params), _trunk(x_sample, params))
```

```python
import functools
import math

import jax
import jax.numpy as jnp
from jax import lax
from jax.experimental import pallas as pl
from jax.experimental.pallas import tpu as pltpu

LANES = 128
D_MODEL = 1024
D_FF = 2816
RMS_EPS = 1e-6
MASK_BIAS = -1e30

ATTN_HEADS = 8
ATTN_HEAD_DIM = 64
ATTN_GROUP_WIDTH = ATTN_HEADS * ATTN_HEAD_DIM
HALF_TAPS = 64
LSE_LANES = LANES // ATTN_HEADS
ATTN_SUB = 128
ATTN_WIN = ATTN_SUB + 2 * HALF_TAPS

HGRN_HEADS = 8
HGRN_DIM = 128
HGRN_TILE = 128
HGRN_UNROLL = 8
HGRN_FINISH_ROWS = 512
_DONE = object()

FF_CHUNK = 256
N_FF_CHUNKS = D_FF // FF_CHUNK

FFN_ROWS = 512
PROJ_ROWS = 256
MIX_ROWS = 512

COL_GATES = 0
COL_QS = 2048
COL_FWD = 3072
COL_BWD = 6144
COL_HI = 9216
COL_HG = 10240
COL_ATTN = 11264
ATTN_QKV_WIDTH = 3 * ATTN_GROUP_WIDTH
MAIN_WIDTH = COL_ATTN + ATTN_QKV_WIDTH
WCOL_GATES, WCOL_HQ, WCOL_HFF, WCOL_HFB, WCOL_HI, WCOL_HG, WCOL_ATTN = (
    0, 2048, 3072, 4096, 5120, 6144, 7168)
PROJ_CHUNK = 512

VMEM_LIMIT = 56 * 1024 * 1024

_NT = (((1,), (1,)), ((), ()))
_TN = (((0,), (0,)), ((), ()))


def _resident(shape):
    zeros = (0,) * len(shape)
    return pl.BlockSpec(shape, lambda *_: zeros, pipeline_mode=pl.Buffered(1))


def _rms(x, g):
    return x * lax.rsqrt(jnp.mean(x * x, axis=-1, keepdims=True) + RMS_EPS) * g


def _swiglu_into(acc_ref, h_ref, wg_ref, wu_ref, wd_ref):
    f32 = jnp.float32

    def gate_up(c):
        cols = slice(c * FF_CHUNK, (c + 1) * FF_CHUNK)
        a = jnp.dot(h_ref[...], wg_ref[:, cols], preferred_element_type=f32)
        b = jnp.dot(h_ref[...], wu_ref[:, cols], preferred_element_type=f32)
        return (a * jax.nn.sigmoid(a) * b).astype(jnp.bfloat16)

    def down(c, act):
        part = jnp.dot(act, wd_ref[c * FF_CHUNK:(c + 1) * FF_CHUNK, :], preferred_element_type=f32)
        if c == 0:
            acc_ref[...] = part
        else:
            acc_ref[...] += part

    act = gate_up(0)
    for c in range(1, N_FF_CHUNKS):
        nxt = gate_up(c)
        down(c - 1, act)
        act = nxt
    down(N_FF_CHUNKS - 1, act)


def _ffn_kernel(x_ref, gin_ref, wg_ref, wu_ref, wd_ref, y_ref, hs_ref, acc_ref):
    x = x_ref[...]
    hs_ref[...] = _rms(x, gin_ref[...]).astype(jnp.bfloat16)
    _swiglu_into(acc_ref, hs_ref, wg_ref, wu_ref, wd_ref)
    y_ref[...] = x + 0.5 * acc_ref[...]


def _ffn_final_kernel(x_ref, gin_ref, gout_ref, wg_ref, wu_ref, wd_ref, y_ref, hs_ref, acc_ref):
    x = x_ref[...]
    hs_ref[...] = _rms(x, gin_ref[...]).astype(jnp.bfloat16)
    _swiglu_into(acc_ref, hs_ref, wg_ref, wu_ref, wd_ref)
    y_ref[...] = _rms(x + 0.5 * acc_ref[...], gout_ref[...])


def _ffn(x2d, gin, weights, gout=None):
    n = x2d.shape[0]
    row = pl.BlockSpec((FFN_ROWS, D_MODEL), lambda i: (i, 0))
    gains = [gin] if gout is None else [gin, gout]
    return pl.pallas_call(
        _ffn_kernel if gout is None else _ffn_final_kernel,
        grid=(n // FFN_ROWS,),
        in_specs=[row] + [_resident((1, D_MODEL))] * len(gains) + [_resident(w.shape) for w in weights],
        out_specs=row,
        out_shape=jax.ShapeDtypeStruct((n, D_MODEL), jnp.float32),
        scratch_shapes=[
            pltpu.VMEM((FFN_ROWS, D_MODEL), jnp.bfloat16),
            pltpu.VMEM((FFN_ROWS, D_MODEL), jnp.float32),
        ],
        compiler_params=pltpu.CompilerParams(
            dimension_semantics=("arbitrary",), vmem_limit_bytes=VMEM_LIMIT),
        name="ffn" if gout is None else "ffn_final",
    )(x2d, *gains, *weights)


def _proj_kernel(x1_ref, gm_ref, lbf_ref, lbb_ref, wm_ref, w4_ref, w16_ref,
                 pm_ref, p4_ref, p16_ref, hs_ref, hb_ref):
    f32, bf16 = jnp.float32, jnp.bfloat16
    hf = _rms(x1_ref[0], gm_ref[...])
    for c in range(D_MODEL // LANES):
        hs_ref[c] = hf[:, c * LANES:(c + 1) * LANES]
    hb_ref[0] = hf.astype(bf16)
    for slot, d in ((1, 4), (2, 16)):
        n = PROJ_ROWS // d
        hb_ref[slot] = jnp.concatenate(
            [jnp.concatenate([hs_ref[c, pl.ds(r, n, stride=d), :]
                              for c in range(D_MODEL // LANES)], axis=1)
             for r in range(d)], axis=0).astype(bf16)

    def project(wcol):
        return jnp.dot(hb_ref[0], wm_ref[:, wcol:wcol + PROJ_CHUNK], preferred_element_type=f32)

    def put(col, val):
        pm_ref[0, :, col:col + PROJ_CHUNK] = val.astype(bf16)

    def plain(wcol, col):
        put(col, project(wcol))

    def silu(wcol, col):
        hq = project(wcol)
        put(col, hq * jax.nn.sigmoid(hq))

    def gate(wcol, col, lb):
        f = lb + (1.0 - lb) * jax.nn.sigmoid(project(wcol))
        log_f = jnp.log(f)
        hi = log_f.astype(bf16)
        put(col, hi)
        put(col + D_MODEL, log_f - hi.astype(f32))
        put(col + 2 * D_MODEL, 1.0 - f)

    chunks = lambda width: range(0, width, PROJ_CHUNK)
    gates = [functools.partial(gate, wcol0 + off, col0 + off, lb_ref[:, off:off + PROJ_CHUNK])
             for wcol0, col0, lb_ref in ((WCOL_HFF, COL_FWD, lbf_ref), (WCOL_HFB, COL_BWD, lbb_ref))
             for off in chunks(D_MODEL)]
    silus = [functools.partial(silu, WCOL_HQ + off, COL_QS + off) for off in chunks(D_MODEL)]
    plains = [functools.partial(plain, wcol0 + off, col0 + off)
              for wcol0, col0, width in ((WCOL_GATES, COL_GATES, 2 * D_MODEL),
                                         (WCOL_HI, COL_HI, D_MODEL), (WCOL_HG, COL_HG, D_MODEL),
                                         (WCOL_ATTN, COL_ATTN, ATTN_QKV_WIDTH))
              for off in chunks(width)]
    heavy = gates + silus
    per_heavy = len(plains) // len(heavy)
    for i, task in enumerate(heavy):
        task()
        for p in plains[i * per_heavy:(i + 1) * per_heavy]:
            p()
    for p in plains[len(heavy) * per_heavy:]:
        p()
    for slot, d, w_ref, p_ref in ((1, 4, w4_ref, p4_ref), (2, 16, w16_ref, p16_ref)):
        n = PROJ_ROWS // d
        for j in range(ATTN_QKV_WIDTH // PROJ_CHUNK):
            cols = slice(j * PROJ_CHUNK, (j + 1) * PROJ_CHUNK)
            res = jnp.dot(hb_ref[slot], w_ref[:, cols], preferred_element_type=f32).astype(bf16)
            for r in range(d):
                p_ref[0, r, :, cols] = res[r * n:(r + 1) * n]


def _proj(x1, gm, lb_f, lb_b, w_main, w_d4, w_d16):
    b, s, _ = x1.shape
    bf16 = jnp.bfloat16

    def dilated(d):
        spec = pl.BlockSpec((1, d, PROJ_ROWS // d, ATTN_QKV_WIDTH), lambda bi, i: (bi, 0, i, 0))
        return spec, jax.ShapeDtypeStruct((b, d, s // d, ATTN_QKV_WIDTH), bf16)

    (spec4, sds4), (spec16, sds16) = dilated(4), dilated(16)
    return pl.pallas_call(
        _proj_kernel,
        grid=(b, s // PROJ_ROWS),
        in_specs=[pl.BlockSpec((1, PROJ_ROWS, D_MODEL), lambda bi, i: (bi, i, 0)),
                  _resident((1, D_MODEL)), _resident((1, D_MODEL)), _resident((1, D_MODEL)),
                  _resident(w_main.shape), _resident(w_d4.shape), _resident(w_d16.shape)],
        out_specs=[pl.BlockSpec((1, PROJ_ROWS, MAIN_WIDTH), lambda bi, i: (bi, i, 0)), spec4, spec16],
        out_shape=[jax.ShapeDtypeStruct((b, s, MAIN_WIDTH), bf16), sds4, sds16],
        scratch_shapes=[pltpu.VMEM((D_MODEL // LANES, PROJ_ROWS, LANES), jnp.float32),
                        pltpu.VMEM((3, PROJ_ROWS, D_MODEL), jnp.bfloat16)],
        compiler_params=pltpu.CompilerParams(
            dimension_semantics=("arbitrary", "arbitrary"), vmem_limit_bytes=VMEM_LIMIT),
        name="proj",
    )(x1, gm, lb_f, lb_b, w_main, w_d4, w_d16)


def _attn_kernel(q_ref, kp_ref, kc_ref, kn_ref, vp_ref, vc_ref, vn_ref, o_ref, lse_ref,
                 *, dilation, sub_len, tq):
    q0 = pl.program_id(1) * tq
    kwin = jnp.concatenate([kp_ref[0], kc_ref[0], kn_ref[0]], axis=0)
    vwin = jnp.concatenate([vp_ref[0], vc_ref[0], vn_ref[0]], axis=0)

    row = lax.broadcasted_iota(jnp.int32, (ATTN_SUB, ATTN_WIN), 0)
    col = lax.broadcasted_iota(jnp.int32, (ATTN_SUB, ATTN_WIN), 1)
    rel = jnp.abs(col - HALF_TAPS - row)
    lane = lax.broadcasted_iota(jnp.int32, (ATTN_SUB, 2 * ATTN_HEAD_DIM), 1)
    first_head = lane < ATTN_HEAD_DIM
    ones = jnp.ones((ATTN_WIN, 2 * ATTN_HEAD_DIM), jnp.bfloat16)

    for sub in range(tq // ATTN_SUB):
        r0 = sub * ATTN_SUB
        kpos = q0 + (r0 - HALF_TAPS) + col
        valid = (rel <= HALF_TAPS) & (kpos >= 0) & (kpos < sub_len)
        base = jnp.where(valid, (-dilation) * rel.astype(jnp.float32), MASK_BIAS)
        lse_packed = None
        for pair in range(ATTN_HEADS // 2):
            lanes = slice(pair * 128, (pair + 1) * 128)
            qp = q_ref[0, r0:r0 + ATTN_SUB, lanes]
            kw = kwin[r0:r0 + ATTN_WIN, lanes]
            vw = vwin[r0:r0 + ATTN_WIN, lanes]
            zero = jnp.zeros_like(qp)
            q2 = jnp.concatenate(
                [jnp.where(first_head, qp, zero), jnp.where(first_head, zero, qp)], axis=0)
            s = lax.dot_general(q2, kw, _NT, preferred_element_type=jnp.float32)
            halves = []
            for half in range(2):
                slope = 2.0 ** (-(2 * pair + half + 1))
                sh = s[half * ATTN_SUB:(half + 1) * ATTN_SUB] + slope * base
                m = jnp.max(sh, axis=-1, keepdims=True)
                halves.append((m, jnp.exp((sh - m).astype(jnp.bfloat16))))
            p2 = jnp.concatenate([halves[0][1], halves[1][1]], axis=0)
            vext = jnp.concatenate([vw, ones], axis=1)
            o2 = jnp.dot(p2, vext, preferred_element_type=jnp.float32)
            outs = []
            for half in range(2):
                rows = slice(half * ATTN_SUB, (half + 1) * ATTN_SUB)
                num = o2[rows, :128]
                den = o2[rows, 128:]
                outs.append(num / den)
                lse = halves[half][0] + jnp.log(den)
                head = 2 * pair + half
                lse_packed = lse if head == 0 else jnp.where(
                    lane >= head * LSE_LANES, lse, lse_packed)
            o_ref[0, r0:r0 + ATTN_SUB, lanes] = jnp.where(
                first_head, outs[0], outs[1]).astype(o_ref.dtype)
        lse_ref[0, r0:r0 + ATTN_SUB, :] = lse_packed


def _attention(qkv, col0, dilation):
    nsub, sub_len, _ = qkv.shape
    tq = next(t for t in (512, 256, ATTN_SUB) if sub_len % t == 0)
    halo_per_tile = tq // HALF_TAPS
    last_halo = sub_len // HALF_TAPS - 1

    def cur(which):
        return lambda si, i: (si, i, col0 + which)

    def prev(which):
        return lambda si, i: (si, jnp.maximum(i * halo_per_tile - 1, 0), col0 + which)

    def nxt(which):
        return lambda si, i: (si, jnp.minimum((i + 1) * halo_per_tile, last_halo), col0 + which)

    main = lambda which: pl.BlockSpec((1, tq, ATTN_GROUP_WIDTH), cur(which))
    halo = lambda fn, which: pl.BlockSpec((1, HALF_TAPS, ATTN_GROUP_WIDTH), fn(which))
    out = lambda width: pl.BlockSpec((1, tq, width), lambda si, i: (si, i, 0))
    return pl.pallas_call(
        functools.partial(_attn_kernel, dilation=dilation, sub_len=sub_len, tq=tq),
        grid=(nsub, sub_len // tq),
        in_specs=[main(0), halo(prev, 1), main(1), halo(nxt, 1), halo(prev, 2), main(2), halo(nxt, 2)],
        out_specs=[out(ATTN_GROUP_WIDTH), out(LANES)],
        out_shape=[jax.ShapeDtypeStruct((nsub, sub_len, ATTN_GROUP_WIDTH), jnp.bfloat16),
                   jax.ShapeDtypeStruct((nsub, sub_len, LANES), jnp.float32)],
        compiler_params=pltpu.CompilerParams(
            dimension_semantics=("arbitrary", "arbitrary"), vmem_limit_bytes=VMEM_LIMIT),
        name=f"attn_d{dilation}",
    )(qkv, qkv, qkv, qkv, qkv, qkv, qkv)


def _hgrn_tile_stages(rows, tri_ref, qs_ref, gate, v_ref, carry, o_ref, *, reverse):
    f32, bf16 = jnp.float32, jnp.bfloat16
    lf_hi_ref, lf_mid_ref, k_ref = gate
    parts = jnp.dot(tri_ref[...],
                    jnp.concatenate([lf_hi_ref[0, rows, :], lf_mid_ref[0, rows, :]], axis=1),
                    preferred_element_type=f32)
    yield
    bcum = parts[:, HGRN_DIM:] + parts[:, :HGRN_DIM]
    mid, last = (HGRN_TILE // 2, 0) if reverse else (HGRN_TILE // 2 - 1, HGRN_TILE - 1)
    b_mid = bcum[mid:mid + 1]
    b_last = bcum[last:last + 1]
    q_t = (qs_ref[0, rows, :].astype(f32) * jnp.exp(bcum - b_mid)).astype(bf16)
    k_mid = k_ref[0, rows, :].astype(f32) * jnp.exp(b_mid - bcum)
    k_end = (k_mid * jnp.exp(b_last - b_mid)).astype(bf16)
    v = v_ref[0, rows, :]
    a = lax.dot_general(q_t, k_mid.astype(bf16), _NT, preferred_element_type=f32)
    update = lax.dot_general(v, k_end, _TN, preferred_element_type=f32)
    yield
    state = carry["state"]
    inter = lax.dot_general(q_t, (state * jnp.exp(b_mid)).astype(bf16), _NT,
                            preferred_element_type=f32)
    carry["state"] = state * jnp.exp(b_last) + update
    yield
    tpos = lax.broadcasted_iota(jnp.int32, (HGRN_TILE, HGRN_TILE), 0)
    spos = lax.broadcasted_iota(jnp.int32, (HGRN_TILE, HGRN_TILE), 1)
    seen = (tpos <= spos) if reverse else (tpos >= spos)
    a = jnp.where(seen, a, 0.0).astype(bf16)
    o_ref[rows, :] = jnp.dot(a, v, preferred_element_type=f32) + inter


def _hgrn_kernel(qs_ref, fhi_ref, fmid_ref, fk_ref, bhi_ref, bmid_ref, bk_ref, v_ref, hg_ref,
                 ng_ref, trif_ref, trib_ref, out_ref, of_ref, ob_ref, stf_ref, stb_ref, *, seq):
    n_tiles = seq // HGRN_TILE
    stf_ref[...] = jnp.zeros_like(stf_ref)
    stb_ref[...] = jnp.zeros_like(stb_ref)
    fwd_gate = (fhi_ref, fmid_ref, fk_ref)
    bwd_gate = (bhi_ref, bmid_ref, bk_ref)

    def step(t, loop_carry):
        fwd = {"state": stf_ref[...]}
        bwd = {"state": stb_ref[...]}
        tiles = []
        for j in range(HGRN_UNROLL):
            tf = t * HGRN_UNROLL + j
            fwd_rows = pl.ds(pl.multiple_of(tf * HGRN_TILE, HGRN_TILE), HGRN_TILE)
            bwd_rows = pl.ds(pl.multiple_of((n_tiles - 1 - tf) * HGRN_TILE, HGRN_TILE), HGRN_TILE)
            tiles.append(_hgrn_tile_stages(fwd_rows, trif_ref, qs_ref, fwd_gate, v_ref, fwd,
                                           of_ref, reverse=False))
            tiles.append(_hgrn_tile_stages(bwd_rows, trib_ref, qs_ref, bwd_gate, v_ref, bwd,
                                           ob_ref, reverse=True))
        while tiles:
            tiles = [g for g in tiles if next(g, _DONE) is not _DONE]
        stf_ref[...] = fwd["state"]
        stb_ref[...] = bwd["state"]
        return loop_carry

    lax.fori_loop(0, n_tiles // HGRN_UNROLL, step, 0)

    ng = ng_ref[...]

    def finish(t, carry):
        rows = pl.ds(pl.multiple_of(t * HGRN_FINISH_ROWS, HGRN_FINISH_ROWS), HGRN_FINISH_ROWS)
        o = of_ref[rows, :] + ob_ref[rows, :]
        hg = hg_ref[0, rows, :].astype(jnp.float32)
        o = _rms(o, ng) * (hg * jax.nn.sigmoid(hg))
        out_ref[0, rows, :] = o.astype(jnp.bfloat16)
        return carry

    lax.fori_loop(0, seq // HGRN_FINISH_ROWS, finish, 0)


def _tile_triangles():
    r = jnp.arange(HGRN_TILE)[:, None]
    c = jnp.arange(HGRN_TILE)[None, :]
    return (r >= c).astype(jnp.bfloat16), (r <= c).astype(jnp.bfloat16)


def _hgrn(p3d, norm_g):
    b, s, _ = p3d.shape
    tri_f, tri_b = _tile_triangles()

    def head_cols(col):
        return pl.BlockSpec((1, s, HGRN_DIM), lambda bi, h: (bi, 0, col // HGRN_DIM + h))

    planes = [COL_QS] + [c0 + i * D_MODEL for c0 in (COL_FWD, COL_BWD) for i in range(3)] + [
        COL_HI, COL_HG]
    return pl.pallas_call(
        functools.partial(_hgrn_kernel, seq=s),
        grid=(b, HGRN_HEADS),
        in_specs=[head_cols(col) for col in planes] + [
            pl.BlockSpec((1, HGRN_DIM), lambda bi, h: (0, h)),
            _resident(tri_f.shape), _resident(tri_b.shape)],
        out_specs=pl.BlockSpec((1, s, HGRN_DIM), lambda bi, h: (bi, 0, h)),
        out_shape=jax.ShapeDtypeStruct((b, s, HGRN_HEADS * HGRN_DIM), jnp.bfloat16),
        scratch_shapes=[
            pltpu.VMEM((s, HGRN_DIM), jnp.float32),
            pltpu.VMEM((s, HGRN_DIM), jnp.float32),
            pltpu.VMEM((HGRN_DIM, HGRN_DIM), jnp.float32),
            pltpu.VMEM((HGRN_DIM, HGRN_DIM), jnp.float32),
        ],
        compiler_params=pltpu.CompilerParams(
            dimension_semantics=("arbitrary", "arbitrary"), vmem_limit_bytes=VMEM_LIMIT),
        name="hgrn2",
    )(*([p3d] * len(planes)), norm_g, tri_f, tri_b)


def _mix_kernel(x1_ref, gates_ref, o0_ref, l0_ref, o4_ref, l4_ref, o16_ref, l16_ref, yb_ref,
                widen_ref, wa_ref, wb_ref, wo_ref, x2_ref, so4_ref, sl4_ref, so16_ref, sl16_ref):
    f32, bf16 = jnp.float32, jnp.bfloat16
    for d, pairs in ((4, ((o4_ref, so4_ref), (l4_ref, sl4_ref))),
                     (16, ((o16_ref, so16_ref), (l16_ref, sl16_ref)))):
        n = MIX_ROWS // d
        for src_ref, dst_ref in pairs:
            for r in range(d):
                rows = src_ref[0, r].astype(f32)
                for c in range(dst_ref.shape[0]):
                    dst_ref[c, pl.ds(r, n, stride=d), :] = rows[:, c * LANES:(c + 1) * LANES]
    tokens = lambda ref: jnp.concatenate([ref[c] for c in range(ref.shape[0])], axis=1)
    l0, l1, l2 = l0_ref[0], sl4_ref[0], sl16_ref[0]
    m = jnp.maximum(jnp.maximum(l0, l1), l2)
    e0, e1, e2 = jnp.exp(l0 - m), jnp.exp(l1 - m), jnp.exp(l2 - m)
    inv = 1.0 / (e0 + e1 + e2)

    def widen(w):
        hi = w.astype(bf16)
        mid = (w - hi.astype(f32)).astype(bf16)
        return jnp.dot(jnp.concatenate([hi, mid], axis=1), widen_ref[...],
                       preferred_element_type=f32)

    ya = (widen(e0 * inv) * o0_ref[0].astype(f32) + widen(e1 * inv) * tokens(so4_ref)
          + widen(e2 * inv) * tokens(so16_ref))
    ma = jnp.dot(ya.astype(bf16), wa_ref[...], preferred_element_type=f32)
    mb = jnp.dot(yb_ref[0], wb_ref[...], preferred_element_type=f32)
    ga = gates_ref[0, :, :D_MODEL].astype(f32)
    gb = gates_ref[0, :, D_MODEL:].astype(f32)
    merged = jax.nn.sigmoid(ga) * ma + jax.nn.sigmoid(gb) * mb
    x2_ref[0] = x1_ref[0] + jnp.dot(merged.astype(bf16), wo_ref[...], preferred_element_type=f32)


def _widen_matrix():
    k = jnp.arange(2 * LANES)[:, None] % LANES
    head = jnp.arange(ATTN_GROUP_WIDTH)[None, :] // ATTN_HEAD_DIM
    return (k == head * LSE_LANES).astype(jnp.bfloat16)


def _mix(x1, pm, attn0, attn4, attn16, yb, wa, wb, wo):
    b, s, _ = x1.shape
    widen = _widen_matrix()
    tok = lambda width: pl.BlockSpec((1, MIX_ROWS, width), lambda bi, i: (bi, i, 0))
    sub = lambda d, width: pl.BlockSpec((1, d, MIX_ROWS // d, width), lambda bi, i: (bi, 0, i, 0))
    planes = lambda width: pltpu.VMEM((width // LANES, MIX_ROWS, LANES), jnp.float32)
    return pl.pallas_call(
        _mix_kernel,
        grid=(b, s // MIX_ROWS),
        in_specs=[tok(D_MODEL), tok(2 * D_MODEL), tok(ATTN_GROUP_WIDTH), tok(LANES),
                  sub(4, ATTN_GROUP_WIDTH), sub(4, LANES), sub(16, ATTN_GROUP_WIDTH), sub(16, LANES),
                  tok(D_MODEL), _resident(widen.shape),
                  _resident(wa.shape), _resident(wb.shape), _resident(wo.shape)],
        out_specs=tok(D_MODEL),
        out_shape=jax.ShapeDtypeStruct((b, s, D_MODEL), jnp.float32),
        scratch_shapes=[planes(ATTN_GROUP_WIDTH), planes(LANES)] * 2,
        compiler_params=pltpu.CompilerParams(
            dimension_semantics=("arbitrary", "arbitrary"), vmem_limit_bytes=VMEM_LIMIT),
        name="mix",
    )(x1, pm, *attn0, *attn4, *attn16, yb, widen, wa, wb, wo)


def _ffn_weights(w_gu, w_down):
    bf16 = jnp.bfloat16
    return w_gu[:, :D_FF].astype(bf16), w_gu[:, D_FF:].astype(bf16), w_down.astype(bf16)


def _in_weights(w_in):
    aw = 3 * ATTN_GROUP_WIDTH
    a_q, a_k, a_v = w_in[:, :aw], w_in[:, aw:2 * aw], w_in[:, 2 * aw:3 * aw]
    rest = w_in[:, 3 * aw:]
    h_q, h_ff, h_fb, h_i, h_g, g_a, g_b = jnp.split(rest, 7, axis=1)
    scale = 1.0 / math.sqrt(ATTN_HEAD_DIM)

    def qkv(g):
        sl = slice(g * ATTN_GROUP_WIDTH, (g + 1) * ATTN_GROUP_WIDTH)
        return [a_q[:, sl] * scale, a_k[:, sl], a_v[:, sl]]

    cat = lambda cols: jnp.concatenate(cols, axis=1).astype(jnp.bfloat16)
    return cat([g_a, g_b, h_q, h_ff, h_fb, h_i, h_g] + qkv(0)), cat(qkv(1)), cat(qkv(2))


def _lower_bound(lb_raw, layer):
    return jnp.cumsum(jax.nn.softmax(lb_raw.astype(jnp.float32), axis=0), axis=0)[layer][None, :]


def _trunk(x, params):
    (g1, gm, ffn1_w, w_in, lb_f, lb_b, hgrn_norm, wa, wb, wo, g2, gf, ffn2_w) = params
    b, s, _ = x.shape
    n = b * s
    x1 = _ffn(x.reshape(n, D_MODEL), g1, ffn1_w).reshape(b, s, D_MODEL)
    pm, p4, p16 = _proj(x1, gm, lb_f, lb_b, *w_in)
    attn0 = _attention(pm, COL_ATTN // ATTN_GROUP_WIDTH, 1)
    dilated = []
    for d, qkv in ((4, p4), (16, p16)):
        o, lse = _attention(qkv.reshape(b * d, s // d, ATTN_QKV_WIDTH), 0, d)
        dilated.append((o.reshape(b, d, s // d, ATTN_GROUP_WIDTH),
                        lse.reshape(b, d, s // d, LANES)))
    yb = _hgrn(pm, hgrn_norm)
    x2 = _mix(x1, pm, attn0, *dilated, yb, wa, wb, wo)
    y = _ffn(x2.reshape(n, D_MODEL), g2, ffn2_w, gout=gf)
    return y.reshape(b, s, D_MODEL)


def kernel(x_prompt, x_sample, ffn1_norm, ffn1_w_gu, ffn1_w_down, mix_norm, w_in, hgrn_lb_fwd,
           hgrn_lb_bwd, hgrn_norm, w_branch_a, w_branch_b, w_out, ffn2_norm, ffn2_w_gu,
           ffn2_w_down, final_norm):
    bf16 = jnp.bfloat16
    layer = 0
    params = (
        ffn1_norm[layer][None, :], mix_norm[layer][None, :],
        _ffn_weights(ffn1_w_gu[layer], ffn1_w_down[layer]),
        _in_weights(w_in[layer]),
        _lower_bound(hgrn_lb_fwd, layer), _lower_bound(hgrn_lb_bwd, layer),
        hgrn_norm[layer][None, :],
        w_branch_a[layer].astype(bf16), w_branch_b[layer].astype(bf16), w_out[layer].astype(bf16),
        ffn2_norm[layer][None, :], final_norm[None, :],
        _ffn_weights(ffn2_w_gu[layer], ffn2_w_down[layer]),
    )
    return (_trunk(x_prompt, params), _trunk(x_sample, params))
```

```python
import functools
import math

import jax
import jax.numpy as jnp
from jax import lax
from jax.experimental import pallas as pl
from jax.experimental.pallas import tpu as pltpu

LANES = 128
D_MODEL = 1024
D_FF = 2816
RMS_EPS = 1e-6
MASK_BIAS = -1e30

ATTN_HEADS = 8
ATTN_HEAD_DIM = 64
ATTN_GROUP_WIDTH = ATTN_HEADS * ATTN_HEAD_DIM
HALF_TAPS = 64
LSE_LANES = LANES // ATTN_HEADS
ATTN_SUB = 128
ATTN_WIN = ATTN_SUB + 2 * HALF_TAPS

HGRN_HEADS = 8
HGRN_DIM = 128
HGRN_TILE = 128
HGRN_UNROLL = 8
HGRN_FINISH_ROWS = 512
_DONE = object()

FF_CHUNK = 256
N_FF_CHUNKS = D_FF // FF_CHUNK

FFN_ROWS = 1024
PROJ_ROWS = 512
MIX_ROWS = 512

COL_GATES = 0
COL_QS = 2048
COL_FWD = 3072
COL_BWD = 6144
COL_HI = 9216
COL_HG = 10240
MAIN_WIDTH = COL_HG + D_MODEL
WCOL_GATES, WCOL_HQ, WCOL_HFF, WCOL_HFB, WCOL_HI, WCOL_HG = 0, 2048, 3072, 4096, 5120, 6144
ATTN_QKV_WIDTH = 3 * ATTN_GROUP_WIDTH
PROJ_CHUNK = 256

VMEM_LIMIT = 56 * 1024 * 1024

_NT = (((1,), (1,)), ((), ()))
_TN = (((0,), (0,)), ((), ()))


def _resident(shape):
    zeros = (0,) * len(shape)
    return pl.BlockSpec(shape, lambda *_: zeros, pipeline_mode=pl.Buffered(1))


def _rms(x, g):
    return x * lax.rsqrt(jnp.mean(x * x, axis=-1, keepdims=True) + RMS_EPS) * g


def _swiglu_into(acc_ref, h_ref, wg_ref, wu_ref, wd_ref):
    f32 = jnp.float32

    def gate_up(c):
        cols = slice(c * FF_CHUNK, (c + 1) * FF_CHUNK)
        a = jnp.dot(h_ref[...], wg_ref[:, cols], preferred_element_type=f32)
        b = jnp.dot(h_ref[...], wu_ref[:, cols], preferred_element_type=f32)
        return (a * jax.nn.sigmoid(a) * b).astype(jnp.bfloat16)

    def down(c, act):
        part = jnp.dot(act, wd_ref[c * FF_CHUNK:(c + 1) * FF_CHUNK, :], preferred_element_type=f32)
        if c == 0:
            acc_ref[...] = part
        else:
            acc_ref[...] += part

    act = gate_up(0)
    for c in range(1, N_FF_CHUNKS):
        nxt = gate_up(c)
        down(c - 1, act)
        act = nxt
    down(N_FF_CHUNKS - 1, act)


def _ffn_kernel(x_ref, gin_ref, wg_ref, wu_ref, wd_ref, y_ref, hs_ref, acc_ref):
    x = x_ref[...]
    hs_ref[...] = _rms(x, gin_ref[...]).astype(jnp.bfloat16)
    _swiglu_into(acc_ref, hs_ref, wg_ref, wu_ref, wd_ref)
    y_ref[...] = x + 0.5 * acc_ref[...]


def _ffn_final_kernel(x_ref, gin_ref, gout_ref, wg_ref, wu_ref, wd_ref, y_ref, hs_ref, acc_ref):
    x = x_ref[...]
    hs_ref[...] = _rms(x, gin_ref[...]).astype(jnp.bfloat16)
    _swiglu_into(acc_ref, hs_ref, wg_ref, wu_ref, wd_ref)
    y_ref[...] = _rms(x + 0.5 * acc_ref[...], gout_ref[...])


def _ffn(x2d, gin, weights, gout=None):
    n = x2d.shape[0]
    row = pl.BlockSpec((FFN_ROWS, D_MODEL), lambda i: (i, 0))
    gains = [gin] if gout is None else [gin, gout]
    return pl.pallas_call(
        _ffn_kernel if gout is None else _ffn_final_kernel,
        grid=(n // FFN_ROWS,),
        in_specs=[row] + [_resident((1, D_MODEL))] * len(gains) + [_resident(w.shape) for w in weights],
        out_specs=row,
        out_shape=jax.ShapeDtypeStruct((n, D_MODEL), jnp.float32),
        scratch_shapes=[
            pltpu.VMEM((FFN_ROWS, D_MODEL), jnp.bfloat16),
            pltpu.VMEM((FFN_ROWS, D_MODEL), jnp.float32),
        ],
        compiler_params=pltpu.CompilerParams(
            dimension_semantics=("arbitrary",), vmem_limit_bytes=VMEM_LIMIT),
        name="ffn" if gout is None else "ffn_final",
    )(x2d, *gains, *weights)


def _proj_main_kernel(x1_ref, gm_ref, lbf_ref, lbb_ref, wm_ref, pm_ref, hb_ref):
    f32, bf16 = jnp.float32, jnp.bfloat16
    hb_ref[...] = _rms(x1_ref[0], gm_ref[...]).astype(bf16)

    def project(wcol):
        return jnp.dot(hb_ref[...], wm_ref[:, wcol:wcol + PROJ_CHUNK], preferred_element_type=f32)

    def put(col, val):
        pm_ref[0, :, col:col + PROJ_CHUNK] = val.astype(bf16)

    def plain(wcol, col):
        put(col, project(wcol))

    def silu(wcol, col):
        hq = project(wcol)
        put(col, hq * jax.nn.sigmoid(hq))

    def gate(wcol, col, lb):
        k = (1.0 - lb) * jax.nn.sigmoid(-project(wcol))
        log_f = jnp.log2(1.0 - k)
        hi = log_f.astype(bf16)
        put(col, hi)
        put(col + D_MODEL, log_f - hi.astype(f32))
        put(col + 2 * D_MODEL, k)

    chunks = lambda width: range(0, width, PROJ_CHUNK)
    gates = [functools.partial(gate, wcol0 + off, col0 + off, lb_ref[:, off:off + PROJ_CHUNK])
             for wcol0, col0, lb_ref in ((WCOL_HFF, COL_FWD, lbf_ref), (WCOL_HFB, COL_BWD, lbb_ref))
             for off in chunks(D_MODEL)]
    silus = [functools.partial(silu, WCOL_HQ + off, COL_QS + off) for off in chunks(D_MODEL)]
    plains = [functools.partial(plain, wcol0 + off, col0 + off)
              for wcol0, col0, width in ((WCOL_GATES, COL_GATES, 2 * D_MODEL),
                                         (WCOL_HI, COL_HI, D_MODEL), (WCOL_HG, COL_HG, D_MODEL))
              for off in chunks(width)]
    heavy = gates + silus
    for i, task in enumerate(heavy):
        task()
        plains[i]()
    for p in plains[len(heavy):]:
        p()


def _proj_main(x1, gm, lb_f, lb_b, w_main):
    b, s, _ = x1.shape
    tok = lambda width: pl.BlockSpec((1, PROJ_ROWS, width), lambda bi, i: (bi, i, 0))
    return pl.pallas_call(
        _proj_main_kernel,
        grid=(b, s // PROJ_ROWS),
        in_specs=[tok(D_MODEL), _resident((1, D_MODEL)), _resident((1, D_MODEL)),
                  _resident((1, D_MODEL)), _resident(w_main.shape)],
        out_specs=tok(MAIN_WIDTH),
        out_shape=jax.ShapeDtypeStruct((b, s, MAIN_WIDTH), jnp.bfloat16),
        scratch_shapes=[pltpu.VMEM((PROJ_ROWS, D_MODEL), jnp.bfloat16)],
        compiler_params=pltpu.CompilerParams(
            dimension_semantics=("arbitrary", "arbitrary"), vmem_limit_bytes=VMEM_LIMIT),
        name="proj_main",
    )(x1, gm, lb_f, lb_b, w_main)


def _proj_attn_kernel(x1_ref, gm_ref, w1_ref, w4_ref, w16_ref, p1_ref, p4_ref, p16_ref,
                      hs_ref, hb_ref):
    f32, bf16 = jnp.float32, jnp.bfloat16
    hf = _rms(x1_ref[0], gm_ref[...])
    for c in range(D_MODEL // LANES):
        hs_ref[c] = hf[:, c * LANES:(c + 1) * LANES]
    hb_ref[0] = hf.astype(bf16)
    for slot, d in ((1, 4), (2, 16)):
        n = PROJ_ROWS // d
        hb_ref[slot] = jnp.concatenate(
            [jnp.concatenate([hs_ref[c, pl.ds(r, n, stride=d), :]
                              for c in range(D_MODEL // LANES)], axis=1)
             for r in range(d)], axis=0).astype(bf16)
    for j in range(ATTN_QKV_WIDTH // PROJ_CHUNK):
        cols = slice(j * PROJ_CHUNK, (j + 1) * PROJ_CHUNK)
        p1_ref[0, :, cols] = jnp.dot(
            hb_ref[0], w1_ref[:, cols], preferred_element_type=f32).astype(bf16)
    for slot, d, w_ref, p_ref in ((1, 4, w4_ref, p4_ref), (2, 16, w16_ref, p16_ref)):
        n = PROJ_ROWS // d
        for j in range(ATTN_QKV_WIDTH // PROJ_CHUNK):
            cols = slice(j * PROJ_CHUNK, (j + 1) * PROJ_CHUNK)
            res = jnp.dot(hb_ref[slot], w_ref[:, cols], preferred_element_type=f32).astype(bf16)
            for r in range(d):
                p_ref[0, r, :, cols] = res[r * n:(r + 1) * n]


def _proj_attn(x1, gm, w_d1, w_d4, w_d16):
    b, s, _ = x1.shape
    bf16 = jnp.bfloat16

    def dilated(d):
        spec = pl.BlockSpec((1, d, PROJ_ROWS // d, ATTN_QKV_WIDTH), lambda bi, i: (bi, 0, i, 0))
        return spec, jax.ShapeDtypeStruct((b, d, s // d, ATTN_QKV_WIDTH), bf16)

    (spec4, sds4), (spec16, sds16) = dilated(4), dilated(16)
    tok = lambda width: pl.BlockSpec((1, PROJ_ROWS, width), lambda bi, i: (bi, i, 0))
    return pl.pallas_call(
        _proj_attn_kernel,
        grid=(b, s // PROJ_ROWS),
        in_specs=[tok(D_MODEL), _resident((1, D_MODEL)),
                  _resident(w_d1.shape), _resident(w_d4.shape), _resident(w_d16.shape)],
        out_specs=[tok(ATTN_QKV_WIDTH), spec4, spec16],
        out_shape=[jax.ShapeDtypeStruct((b, s, ATTN_QKV_WIDTH), bf16), sds4, sds16],
        scratch_shapes=[pltpu.VMEM((D_MODEL // LANES, PROJ_ROWS, LANES), jnp.float32),
                        pltpu.VMEM((3, PROJ_ROWS, D_MODEL), jnp.bfloat16)],
        compiler_params=pltpu.CompilerParams(
            dimension_semantics=("arbitrary", "arbitrary"), vmem_limit_bytes=VMEM_LIMIT),
        name="proj_attn",
    )(x1, gm, w_d1, w_d4, w_d16)


def _attn_kernel(q_ref, kp_ref, kc_ref, kn_ref, vp_ref, vc_ref, vn_ref, o_ref, lse_ref,
                 *, dilation, sub_len, tq):
    q0 = pl.program_id(1) * tq
    kwin = jnp.concatenate([kp_ref[0], kc_ref[0], kn_ref[0]], axis=0)
    vwin = jnp.concatenate([vp_ref[0], vc_ref[0], vn_ref[0]], axis=0)

    row = lax.broadcasted_iota(jnp.int32, (ATTN_SUB, ATTN_WIN), 0)
    col = lax.broadcasted_iota(jnp.int32, (ATTN_SUB, ATTN_WIN), 1)
    rel = jnp.abs(col - HALF_TAPS - row)
    lane = lax.broadcasted_iota(jnp.int32, (ATTN_SUB, 2 * ATTN_HEAD_DIM), 1)
    first_head = lane < ATTN_HEAD_DIM
    ones = jnp.ones((ATTN_WIN, 2 * ATTN_HEAD_DIM), jnp.bfloat16)

    for sub in range(tq // ATTN_SUB):
        r0 = sub * ATTN_SUB
        kpos = q0 + (r0 - HALF_TAPS) + col
        valid = (rel <= HALF_TAPS) & (kpos >= 0) & (kpos < sub_len)
        base = jnp.where(valid, (-dilation) * rel.astype(jnp.float32), MASK_BIAS)
        lse_packed = None
        for pair in range(ATTN_HEADS // 2):
            lanes = slice(pair * 128, (pair + 1) * 128)
            qp = q_ref[0, r0:r0 + ATTN_SUB, lanes]
            kw = kwin[r0:r0 + ATTN_WIN, lanes]
            vw = vwin[r0:r0 + ATTN_WIN, lanes]
            zero = jnp.zeros_like(qp)
            q2 = jnp.concatenate(
                [jnp.where(first_head, qp, zero), jnp.where(first_head, zero, qp)], axis=0)
            s = lax.dot_general(q2, kw, _NT, preferred_element_type=jnp.float32)
            halves = []
            for half in range(2):
                slope = 2.0 ** (-(2 * pair + half + 1))
                sh = s[half * ATTN_SUB:(half + 1) * ATTN_SUB] + slope * base
                m = jnp.max(sh, axis=-1, keepdims=True)
                halves.append((m, jnp.exp((sh - m).astype(jnp.bfloat16))))
            p2 = jnp.concatenate([halves[0][1], halves[1][1]], axis=0)
            vext = jnp.concatenate([vw, ones], axis=1)
            o2 = jnp.dot(p2, vext, preferred_element_type=jnp.float32)
            outs = []
            for half in range(2):
                rows = slice(half * ATTN_SUB, (half + 1) * ATTN_SUB)
                num = o2[rows, :128]
                den = o2[rows, 128:]
                outs.append(num / den)
                lse = halves[half][0] + jnp.log(den)
                head = 2 * pair + half
                lse_packed = lse if head == 0 else jnp.where(
                    lane >= head * LSE_LANES, lse, lse_packed)
            o_ref[0, r0:r0 + ATTN_SUB, lanes] = jnp.where(
                first_head, outs[0], outs[1]).astype(o_ref.dtype)
        lse_ref[0, r0:r0 + ATTN_SUB, :] = lse_packed


def _attention(qkv, col0, dilation):
    nsub, sub_len, _ = qkv.shape
    tq = next(t for t in (512, 256, ATTN_SUB) if sub_len % t == 0)
    halo_per_tile = tq // HALF_TAPS
    last_halo = sub_len // HALF_TAPS - 1

    def cur(which):
        return lambda si, i: (si, i, col0 + which)

    def prev(which):
        return lambda si, i: (si, jnp.maximum(i * halo_per_tile - 1, 0), col0 + which)

    def nxt(which):
        return lambda si, i: (si, jnp.minimum((i + 1) * halo_per_tile, last_halo), col0 + which)

    main = lambda which: pl.BlockSpec((1, tq, ATTN_GROUP_WIDTH), cur(which))
    halo = lambda fn, which: pl.BlockSpec((1, HALF_TAPS, ATTN_GROUP_WIDTH), fn(which))
    out = lambda width: pl.BlockSpec((1, tq, width), lambda si, i: (si, i, 0))
    return pl.pallas_call(
        functools.partial(_attn_kernel, dilation=dilation, sub_len=sub_len, tq=tq),
        grid=(nsub, sub_len // tq),
        in_specs=[main(0), halo(prev, 1), main(1), halo(nxt, 1), halo(prev, 2), main(2), halo(nxt, 2)],
        out_specs=[out(ATTN_GROUP_WIDTH), out(LANES)],
        out_shape=[jax.ShapeDtypeStruct((nsub, sub_len, ATTN_GROUP_WIDTH), jnp.bfloat16),
                   jax.ShapeDtypeStruct((nsub, sub_len, LANES), jnp.float32)],
        compiler_params=pltpu.CompilerParams(
            dimension_semantics=("arbitrary", "arbitrary"), vmem_limit_bytes=VMEM_LIMIT),
        name=f"attn_d{dilation}",
    )(qkv, qkv, qkv, qkv, qkv, qkv, qkv)


def _hgrn_tile_stages(rows, tri_ref, qs_ref, gate, v_ref, carry, o_ref, *, reverse):
    f32, bf16 = jnp.float32, jnp.bfloat16
    lf_hi_ref, lf_mid_ref, k_ref = gate
    parts = jnp.dot(tri_ref[...],
                    jnp.concatenate([lf_hi_ref[0, rows, :], lf_mid_ref[0, rows, :]], axis=1),
                    preferred_element_type=f32)
    yield
    bcum = parts[:, HGRN_DIM:] + parts[:, :HGRN_DIM]
    mid, last = (HGRN_TILE // 2, 0) if reverse else (HGRN_TILE // 2 - 1, HGRN_TILE - 1)
    b_mid = bcum[mid:mid + 1]
    b_last = bcum[last:last + 1]
    decay = jnp.exp2(bcum - b_mid)
    q_t = (qs_ref[0, rows, :].astype(f32) * decay).astype(bf16)
    k_mid = k_ref[0, rows, :].astype(f32) / decay
    k_end = (k_mid * jnp.exp2(b_last - b_mid)).astype(bf16)
    v = v_ref[0, rows, :]
    a = lax.dot_general(q_t, k_mid.astype(bf16), _NT, preferred_element_type=f32)
    update = lax.dot_general(v, k_end, _TN, preferred_element_type=f32)
    yield
    state = carry["state"]
    inter = lax.dot_general(q_t, (state * jnp.exp2(b_mid)).astype(bf16), _NT,
                            preferred_element_type=f32)
    carry["state"] = state * jnp.exp2(b_last) + update
    yield
    tpos = lax.broadcasted_iota(jnp.int32, (HGRN_TILE, HGRN_TILE), 0)
    spos = lax.broadcasted_iota(jnp.int32, (HGRN_TILE, HGRN_TILE), 1)
    seen = (tpos <= spos) if reverse else (tpos >= spos)
    a = jnp.where(seen, a, 0.0).astype(bf16)
    o_ref[rows, :] = jnp.dot(a, v, preferred_element_type=f32) + inter


def _hgrn_kernel(qs_ref, fhi_ref, fmid_ref, fk_ref, bhi_ref, bmid_ref, bk_ref, v_ref, hg_ref,
                 ng_ref, trif_ref, trib_ref, out_ref, of_ref, ob_ref, stf_ref, stb_ref, *, seq):
    n_tiles = seq // HGRN_TILE
    stf_ref[...] = jnp.zeros_like(stf_ref)
    stb_ref[...] = jnp.zeros_like(stb_ref)
    fwd_gate = (fhi_ref, fmid_ref, fk_ref)
    bwd_gate = (bhi_ref, bmid_ref, bk_ref)

    def step(t, loop_carry):
        fwd = {"state": stf_ref[...]}
        bwd = {"state": stb_ref[...]}
        tiles = []
        for j in range(HGRN_UNROLL):
            tf = t * HGRN_UNROLL + j
            fwd_rows = pl.ds(pl.multiple_of(tf * HGRN_TILE, HGRN_TILE), HGRN_TILE)
            bwd_rows = pl.ds(pl.multiple_of((n_tiles - 1 - tf) * HGRN_TILE, HGRN_TILE), HGRN_TILE)
            tiles.append(_hgrn_tile_stages(fwd_rows, trif_ref, qs_ref, fwd_gate, v_ref, fwd,
                                           of_ref, reverse=False))
            tiles.append(_hgrn_tile_stages(bwd_rows, trib_ref, qs_ref, bwd_gate, v_ref, bwd,
                                           ob_ref, reverse=True))
        while tiles:
            tiles = [g for g in tiles if next(g, _DONE) is not _DONE]
        stf_ref[...] = fwd["state"]
        stb_ref[...] = bwd["state"]
        return loop_carry

    lax.fori_loop(0, n_tiles // HGRN_UNROLL, step, 0)

    ng = ng_ref[...]

    def finish(t, carry):
        rows = pl.ds(pl.multiple_of(t * HGRN_FINISH_ROWS, HGRN_FINISH_ROWS), HGRN_FINISH_ROWS)
        o = of_ref[rows, :] + ob_ref[rows, :]
        hg = hg_ref[0, rows, :].astype(jnp.float32)
        o = _rms(o, ng) * (hg * jax.nn.sigmoid(hg))
        out_ref[0, rows, :] = o.astype(jnp.bfloat16)
        return carry

    lax.fori_loop(0, seq // HGRN_FINISH_ROWS, finish, 0)


def _tile_triangles():
    r = jnp.arange(HGRN_TILE)[:, None]
    c = jnp.arange(HGRN_TILE)[None, :]
    return (r >= c).astype(jnp.bfloat16), (r <= c).astype(jnp.bfloat16)


def _hgrn(p3d, norm_g):
    b, s, _ = p3d.shape
    tri_f, tri_b = _tile_triangles()

    def head_cols(col):
        return pl.BlockSpec((1, s, HGRN_DIM), lambda bi, h: (bi, 0, col // HGRN_DIM + h))

    planes = [COL_QS] + [c0 + i * D_MODEL for c0 in (COL_FWD, COL_BWD) for i in range(3)] + [
        COL_HI, COL_HG]
    return pl.pallas_call(
        functools.partial(_hgrn_kernel, seq=s),
        grid=(b, HGRN_HEADS),
        in_specs=[head_cols(col) for col in planes] + [
            pl.BlockSpec((1, HGRN_DIM), lambda bi, h: (0, h)),
            _resident(tri_f.shape), _resident(tri_b.shape)],
        out_specs=pl.BlockSpec((1, s, HGRN_DIM), lambda bi, h: (bi, 0, h)),
        out_shape=jax.ShapeDtypeStruct((b, s, HGRN_HEADS * HGRN_DIM), jnp.bfloat16),
        scratch_shapes=[
            pltpu.VMEM((s, HGRN_DIM), jnp.float32),
            pltpu.VMEM((s, HGRN_DIM), jnp.float32),
            pltpu.VMEM((HGRN_DIM, HGRN_DIM), jnp.float32),
            pltpu.VMEM((HGRN_DIM, HGRN_DIM), jnp.float32),
        ],
        compiler_params=pltpu.CompilerParams(
            dimension_semantics=("arbitrary", "arbitrary"), vmem_limit_bytes=VMEM_LIMIT),
        name="hgrn2",
    )(*([p3d] * len(planes)), norm_g, tri_f, tri_b)


def _mix_kernel(x1_ref, gates_ref, o0_ref, l0_ref, o4_ref, l4_ref, o16_ref, l16_ref, yb_ref,
                widen_ref, wa_ref, wb_ref, wo_ref, x2_ref, so4_ref, sl4_ref, so16_ref, sl16_ref):
    f32, bf16 = jnp.float32, jnp.bfloat16
    for d, pairs in ((4, ((o4_ref, so4_ref), (l4_ref, sl4_ref))),
                     (16, ((o16_ref, so16_ref), (l16_ref, sl16_ref)))):
        n = MIX_ROWS // d
        for src_ref, dst_ref in pairs:
            for r in range(d):
                rows = src_ref[0, r].astype(f32)
                for c in range(dst_ref.shape[0]):
                    dst_ref[c, pl.ds(r, n, stride=d), :] = rows[:, c * LANES:(c + 1) * LANES]
    tokens = lambda ref: jnp.concatenate([ref[c] for c in range(ref.shape[0])], axis=1)
    l0, l1, l2 = l0_ref[0], sl4_ref[0], sl16_ref[0]
    m = jnp.maximum(jnp.maximum(l0, l1), l2)
    e0, e1, e2 = jnp.exp(l0 - m), jnp.exp(l1 - m), jnp.exp(l2 - m)
    inv = 1.0 / (e0 + e1 + e2)

    def widen(w):
        hi = w.astype(bf16)
        mid = (w - hi.astype(f32)).astype(bf16)
        return jnp.dot(jnp.concatenate([hi, mid], axis=1), widen_ref[...],
                       preferred_element_type=f32)

    ya = (widen(e0 * inv) * o0_ref[0].astype(f32) + widen(e1 * inv) * tokens(so4_ref)
          + widen(e2 * inv) * tokens(so16_ref))
    ma = jnp.dot(ya.astype(bf16), wa_ref[...], preferred_element_type=f32)
    mb = jnp.dot(yb_ref[0], wb_ref[...], preferred_element_type=f32)
    ga = gates_ref[0, :, :D_MODEL].astype(f32)
    gb = gates_ref[0, :, D_MODEL:].astype(f32)
    merged = jax.nn.sigmoid(ga) * ma + jax.nn.sigmoid(gb) * mb
    x2_ref[0] = x1_ref[0] + jnp.dot(merged.astype(bf16), wo_ref[...], preferred_element_type=f32)


def _widen_matrix():
    k = jnp.arange(2 * LANES)[:, None] % LANES
    head = jnp.arange(ATTN_GROUP_WIDTH)[None, :] // ATTN_HEAD_DIM
    return (k == head * LSE_LANES).astype(jnp.bfloat16)


def _mix(x1, pm, attn0, attn4, attn16, yb, wa, wb, wo):
    b, s, _ = x1.shape
    widen = _widen_matrix()
    tok = lambda width: pl.BlockSpec((1, MIX_ROWS, width), lambda bi, i: (bi, i, 0))
    sub = lambda d, width: pl.BlockSpec((1, d, MIX_ROWS // d, width), lambda bi, i: (bi, 0, i, 0))
    planes = lambda width: pltpu.VMEM((width // LANES, MIX_ROWS, LANES), jnp.float32)
    return pl.pallas_call(
        _mix_kernel,
        grid=(b, s // MIX_ROWS),
        in_specs=[tok(D_MODEL), tok(2 * D_MODEL), tok(ATTN_GROUP_WIDTH), tok(LANES),
                  sub(4, ATTN_GROUP_WIDTH), sub(4, LANES), sub(16, ATTN_GROUP_WIDTH), sub(16, LANES),
                  tok(D_MODEL), _resident(widen.shape),
                  _resident(wa.shape), _resident(wb.shape), _resident(wo.shape)],
        out_specs=tok(D_MODEL),
        out_shape=jax.ShapeDtypeStruct((b, s, D_MODEL), jnp.float32),
        scratch_shapes=[planes(ATTN_GROUP_WIDTH), planes(LANES)] * 2,
        compiler_params=pltpu.CompilerParams(
            dimension_semantics=("arbitrary", "arbitrary"), vmem_limit_bytes=VMEM_LIMIT),
        name="mix",
    )(x1, pm, *attn0, *attn4, *attn16, yb, widen, wa, wb, wo)


def _ffn_weights(w_gu, w_down):
    bf16 = jnp.bfloat16
    return w_gu[:, :D_FF].astype(bf16), w_gu[:, D_FF:].astype(bf16), w_down.astype(bf16)


def _in_weights(w_in):
    aw = 3 * ATTN_GROUP_WIDTH
    a_q, a_k, a_v = w_in[:, :aw], w_in[:, aw:2 * aw], w_in[:, 2 * aw:3 * aw]
    rest = w_in[:, 3 * aw:]
    h_q, h_ff, h_fb, h_i, h_g, g_a, g_b = jnp.split(rest, 7, axis=1)
    q_scale = 1.0 / math.sqrt(ATTN_HEAD_DIM)

    def qkv(g):
        sl = slice(g * ATTN_GROUP_WIDTH, (g + 1) * ATTN_GROUP_WIDTH)
        return [a_q[:, sl] * q_scale, a_k[:, sl], a_v[:, sl]]

    cat = lambda cols: jnp.concatenate(cols, axis=1).astype(jnp.bfloat16)
    return cat([g_a, g_b, h_q, h_ff, h_fb, h_i, h_g]), cat(qkv(0)), cat(qkv(1)), cat(qkv(2))


def _lower_bound(lb_raw, layer):
    return jnp.cumsum(jax.nn.softmax(lb_raw.astype(jnp.float32), axis=0), axis=0)[layer][None, :]


def _trunk(x, params):
    (g1, gm, ffn1_w, w_in, lb_f, lb_b, hgrn_norm, wa, wb, wo, g2, gf, ffn2_w) = params
    b, s, _ = x.shape
    n = b * s
    x1 = _ffn(x.reshape(n, D_MODEL), g1, ffn1_w).reshape(b, s, D_MODEL)
    pm = _proj_main(x1, gm, lb_f, lb_b, w_in[0])
    p1, p4, p16 = _proj_attn(x1, gm, *w_in[1:])
    attn0 = _attention(p1, 0, 1)
    dilated = []
    for d, qkv in ((4, p4), (16, p16)):
        o, lse = _attention(qkv.reshape(b * d, s // d, ATTN_QKV_WIDTH), 0, d)
        dilated.append((o.reshape(b, d, s // d, ATTN_GROUP_WIDTH),
                        lse.reshape(b, d, s // d, LANES)))
    yb = _hgrn(pm, hgrn_norm)
    x2 = _mix(x1, pm, attn0, *dilated, yb, wa, wb, wo)
    y = _ffn(x2.reshape(n, D_MODEL), g2, ffn2_w, gout=gf)
    return y.reshape(b, s, D_MODEL)


def kernel(x_prompt, x_sample, ffn1_norm, ffn1_w_gu, ffn1_w_down, mix_norm, w_in, hgrn_lb_fwd,
           hgrn_lb_bwd, hgrn_norm, w_branch_a, w_branch_b, w_out, ffn2_norm, ffn2_w_gu,
           ffn2_w_down, final_norm):
    bf16 = jnp.bfloat16
    layer = 0
    params = (
        ffn1_norm[layer][None, :], mix_norm[layer][None, :],
        _ffn_weights(ffn1_w_gu[layer], ffn1_w_down[layer]),
        _in_weights(w_in[layer]),
        _lower_bound(hgrn_lb_fwd, layer), _lower_bound(hgrn_lb_bwd, layer),
        hgrn_norm[layer][None, :],
        w_branch_a[layer].astype(bf16), w_branch_b[layer].astype(bf16), w_out[layer].astype(bf16),
        ffn2_norm[layer][None, :], final_norm[None, :],
        _ffn_weights(ffn2_w_gu[layer], ffn2_w_down[layer]),
    )
    return (_trunk(x_prompt, params), _trunk(x_sample, params))
```

```python
import functools
import math

import jax
import jax.numpy as jnp
from jax import lax
from jax.experimental import pallas as pl
from jax.experimental.pallas import tpu as pltpu

LANES = 128
D_MODEL = 1024
D_FF = 2816
RMS_EPS = 1e-6
MASK_BIAS = -1e30

ATTN_HEADS = 8
ATTN_HEAD_DIM = 64
ATTN_GROUP_WIDTH = ATTN_HEADS * ATTN_HEAD_DIM
HALF_TAPS = 64
LSE_LANES = LANES // ATTN_HEADS
ATTN_SUB = 128
ATTN_WIN = ATTN_SUB + 2 * HALF_TAPS

HGRN_HEADS = 8
HGRN_DIM = 128
HGRN_TILE = 128
HGRN_UNROLL = 8
_DONE = object()

FF_CHUNK = 256
N_FF_CHUNKS = D_FF // FF_CHUNK

FFN_ROWS = 1024
PROJ_ROWS = 512
MIX_ROWS = 512

COL_GATES = 0
COL_QS = 2048
COL_FWD = 3072
COL_BWD = 6144
COL_HI = 9216
COL_HG = 10240
MAIN_WIDTH = COL_HG + D_MODEL
WCOL_GATES, WCOL_HQ, WCOL_HFF, WCOL_HFB, WCOL_HI, WCOL_HG = 0, 2048, 3072, 4096, 5120, 6144
ATTN_QKV_WIDTH = 3 * ATTN_GROUP_WIDTH
PROJ_CHUNK = 256

VMEM_LIMIT = 56 * 1024 * 1024

_NT = (((1,), (1,)), ((), ()))
_TN = (((0,), (0,)), ((), ()))


def _resident(shape):
    zeros = (0,) * len(shape)
    return pl.BlockSpec(shape, lambda *_: zeros, pipeline_mode=pl.Buffered(1))


def _rms(x, g):
    return x * lax.rsqrt(jnp.mean(x * x, axis=-1, keepdims=True) + RMS_EPS) * g


def _swiglu_into(acc_ref, h_ref, wg_ref, wu_ref, wd_ref):
    f32 = jnp.float32

    def gate_up(c):
        cols = slice(c * FF_CHUNK, (c + 1) * FF_CHUNK)
        a = jnp.dot(h_ref[...], wg_ref[:, cols], preferred_element_type=f32)
        b = jnp.dot(h_ref[...], wu_ref[:, cols], preferred_element_type=f32)
        return (a * jax.nn.sigmoid(a) * b).astype(jnp.bfloat16)

    def down(c, act):
        part = jnp.dot(act, wd_ref[c * FF_CHUNK:(c + 1) * FF_CHUNK, :], preferred_element_type=f32)
        if c == 0:
            acc_ref[...] = part
        else:
            acc_ref[...] += part

    act = gate_up(0)
    for c in range(1, N_FF_CHUNKS):
        nxt = gate_up(c)
        down(c - 1, act)
        act = nxt
    down(N_FF_CHUNKS - 1, act)


def _ffn_kernel(x_ref, gin_ref, wg_ref, wu_ref, wd_ref, y_ref, hs_ref, acc_ref):
    x = x_ref[...]
    hs_ref[...] = _rms(x, gin_ref[...]).astype(jnp.bfloat16)
    _swiglu_into(acc_ref, hs_ref, wg_ref, wu_ref, wd_ref)
    y_ref[...] = x + 0.5 * acc_ref[...]


def _ffn_final_kernel(x_ref, gin_ref, gout_ref, wg_ref, wu_ref, wd_ref, y_ref, hs_ref, acc_ref):
    x = x_ref[...]
    hs_ref[...] = _rms(x, gin_ref[...]).astype(jnp.bfloat16)
    _swiglu_into(acc_ref, hs_ref, wg_ref, wu_ref, wd_ref)
    y_ref[...] = _rms(x + 0.5 * acc_ref[...], gout_ref[...])


def _ffn(x2d, gin, weights, gout=None):
    n = x2d.shape[0]
    row = pl.BlockSpec((FFN_ROWS, D_MODEL), lambda i: (i, 0))
    gains = [gin] if gout is None else [gin, gout]
    return pl.pallas_call(
        _ffn_kernel if gout is None else _ffn_final_kernel,
        grid=(n // FFN_ROWS,),
        in_specs=[row] + [_resident((1, D_MODEL))] * len(gains) + [_resident(w.shape) for w in weights],
        out_specs=row,
        out_shape=jax.ShapeDtypeStruct((n, D_MODEL), jnp.float32),
        scratch_shapes=[
            pltpu.VMEM((FFN_ROWS, D_MODEL), jnp.bfloat16),
            pltpu.VMEM((FFN_ROWS, D_MODEL), jnp.float32),
        ],
        compiler_params=pltpu.CompilerParams(
            dimension_semantics=("arbitrary",), vmem_limit_bytes=VMEM_LIMIT),
        name="ffn" if gout is None else "ffn_final",
    )(x2d, *gains, *weights)


def _proj_main_kernel(x1_ref, gm_ref, lbf_ref, lbb_ref, wm_ref, pm_ref, hb_ref):
    f32, bf16 = jnp.float32, jnp.bfloat16
    hb_ref[...] = _rms(x1_ref[0], gm_ref[...]).astype(bf16)

    def project(wcol):
        return jnp.dot(hb_ref[...], wm_ref[:, wcol:wcol + PROJ_CHUNK], preferred_element_type=f32)

    def put(col, val):
        pm_ref[0, :, col:col + PROJ_CHUNK] = val.astype(bf16)

    def plain(wcol, col):
        put(col, project(wcol))

    def silu(wcol, col):
        hq = project(wcol)
        put(col, hq * jax.nn.sigmoid(hq))

    def gate(wcol, col, lb):
        k = (1.0 - lb) * jax.nn.sigmoid(-project(wcol))
        log_f = jnp.log2(1.0 - k)
        hi = log_f.astype(bf16)
        put(col, hi)
        put(col + D_MODEL, log_f - hi.astype(f32))
        put(col + 2 * D_MODEL, k)

    chunks = lambda width: range(0, width, PROJ_CHUNK)
    gates = [functools.partial(gate, wcol0 + off, col0 + off, lb_ref[:, off:off + PROJ_CHUNK])
             for wcol0, col0, lb_ref in ((WCOL_HFF, COL_FWD, lbf_ref), (WCOL_HFB, COL_BWD, lbb_ref))
             for off in chunks(D_MODEL)]
    silus = [functools.partial(silu, wcol0 + off, col0 + off)
             for wcol0, col0 in ((WCOL_HQ, COL_QS), (WCOL_HG, COL_HG)) for off in chunks(D_MODEL)]
    plains = [functools.partial(plain, wcol0 + off, col0 + off)
              for wcol0, col0, width in ((WCOL_GATES, COL_GATES, 2 * D_MODEL),
                                         (WCOL_HI, COL_HI, D_MODEL))
              for off in chunks(width)]
    heavy = gates + silus
    for i, task in enumerate(heavy):
        task()
        if i < len(plains):
            plains[i]()
    for p in plains[len(heavy):]:
        p()


def _proj_main(x1, gm, lb_f, lb_b, w_main):
    b, s, _ = x1.shape
    tok = lambda width: pl.BlockSpec((1, PROJ_ROWS, width), lambda bi, i: (bi, i, 0))
    return pl.pallas_call(
        _proj_main_kernel,
        grid=(b, s // PROJ_ROWS),
        in_specs=[tok(D_MODEL), _resident((1, D_MODEL)), _resident((1, D_MODEL)),
                  _resident((1, D_MODEL)), _resident(w_main.shape)],
        out_specs=tok(MAIN_WIDTH),
        out_shape=jax.ShapeDtypeStruct((b, s, MAIN_WIDTH), jnp.bfloat16),
        scratch_shapes=[pltpu.VMEM((PROJ_ROWS, D_MODEL), jnp.bfloat16)],
        compiler_params=pltpu.CompilerParams(
            dimension_semantics=("arbitrary", "arbitrary"), vmem_limit_bytes=VMEM_LIMIT),
        name="proj_main",
    )(x1, gm, lb_f, lb_b, w_main)


def _proj_attn_kernel(x1_ref, gm_ref, w1_ref, w4_ref, w16_ref, p1_ref, p4_ref, p16_ref,
                      hs_ref, hb_ref):
    f32, bf16 = jnp.float32, jnp.bfloat16
    hf = _rms(x1_ref[0], gm_ref[...])
    for c in range(D_MODEL // LANES):
        hs_ref[c] = hf[:, c * LANES:(c + 1) * LANES]
    hb_ref[0] = hf.astype(bf16)
    for slot, d in ((1, 4), (2, 16)):
        n = PROJ_ROWS // d
        hb_ref[slot] = jnp.concatenate(
            [jnp.concatenate([hs_ref[c, pl.ds(r, n, stride=d), :]
                              for c in range(D_MODEL // LANES)], axis=1)
             for r in range(d)], axis=0).astype(bf16)
    for j in range(ATTN_QKV_WIDTH // PROJ_CHUNK):
        cols = slice(j * PROJ_CHUNK, (j + 1) * PROJ_CHUNK)
        p1_ref[0, :, cols] = jnp.dot(
            hb_ref[0], w1_ref[:, cols], preferred_element_type=f32).astype(bf16)
    for slot, d, w_ref, p_ref in ((1, 4, w4_ref, p4_ref), (2, 16, w16_ref, p16_ref)):
        n = PROJ_ROWS // d
        for j in range(ATTN_QKV_WIDTH // PROJ_CHUNK):
            cols = slice(j * PROJ_CHUNK, (j + 1) * PROJ_CHUNK)
            res = jnp.dot(hb_ref[slot], w_ref[:, cols], preferred_element_type=f32).astype(bf16)
            for r in range(d):
                p_ref[0, r, :, cols] = res[r * n:(r + 1) * n]


def _proj_attn(x1, gm, w_d1, w_d4, w_d16):
    b, s, _ = x1.shape
    bf16 = jnp.bfloat16

    def dilated(d):
        spec = pl.BlockSpec((1, d, PROJ_ROWS // d, ATTN_QKV_WIDTH), lambda bi, i: (bi, 0, i, 0))
        return spec, jax.ShapeDtypeStruct((b, d, s // d, ATTN_QKV_WIDTH), bf16)

    (spec4, sds4), (spec16, sds16) = dilated(4), dilated(16)
    tok = lambda width: pl.BlockSpec((1, PROJ_ROWS, width), lambda bi, i: (bi, i, 0))
    return pl.pallas_call(
        _proj_attn_kernel,
        grid=(b, s // PROJ_ROWS),
        in_specs=[tok(D_MODEL), _resident((1, D_MODEL)),
                  _resident(w_d1.shape), _resident(w_d4.shape), _resident(w_d16.shape)],
        out_specs=[tok(ATTN_QKV_WIDTH), spec4, spec16],
        out_shape=[jax.ShapeDtypeStruct((b, s, ATTN_QKV_WIDTH), bf16), sds4, sds16],
        scratch_shapes=[pltpu.VMEM((D_MODEL // LANES, PROJ_ROWS, LANES), jnp.float32),
                        pltpu.VMEM((3, PROJ_ROWS, D_MODEL), jnp.bfloat16)],
        compiler_params=pltpu.CompilerParams(
            dimension_semantics=("arbitrary", "arbitrary"), vmem_limit_bytes=VMEM_LIMIT),
        name="proj_attn",
    )(x1, gm, w_d1, w_d4, w_d16)


def _attn_kernel(q_ref, kp_ref, kc_ref, kn_ref, vp_ref, vc_ref, vn_ref, o_ref, lse_ref,
                 *, dilation, sub_len, tq):
    q0 = pl.program_id(1) * tq
    kwin = jnp.concatenate([kp_ref[0], kc_ref[0], kn_ref[0]], axis=0)
    vwin = jnp.concatenate([vp_ref[0], vc_ref[0], vn_ref[0]], axis=0)

    row = lax.broadcasted_iota(jnp.int32, (ATTN_SUB, ATTN_WIN), 0)
    col = lax.broadcasted_iota(jnp.int32, (ATTN_SUB, ATTN_WIN), 1)
    rel = jnp.abs(col - HALF_TAPS - row)
    lane = lax.broadcasted_iota(jnp.int32, (ATTN_SUB, 2 * ATTN_HEAD_DIM), 1)
    first_head = lane < ATTN_HEAD_DIM
    ones = jnp.ones((ATTN_WIN, 2 * ATTN_HEAD_DIM), jnp.bfloat16)

    for sub in range(tq // ATTN_SUB):
        r0 = sub * ATTN_SUB
        kpos = q0 + (r0 - HALF_TAPS) + col
        valid = (rel <= HALF_TAPS) & (kpos >= 0) & (kpos < sub_len)
        base = jnp.where(valid, (-dilation) * rel.astype(jnp.float32), MASK_BIAS)
        lse_packed = None
        for pair in range(ATTN_HEADS // 2):
            lanes = slice(pair * 128, (pair + 1) * 128)
            qp = q_ref[0, r0:r0 + ATTN_SUB, lanes]
            kw = kwin[r0:r0 + ATTN_WIN, lanes]
            vw = vwin[r0:r0 + ATTN_WIN, lanes]
            zero = jnp.zeros_like(qp)
            q2 = jnp.concatenate(
                [jnp.where(first_head, qp, zero), jnp.where(first_head, zero, qp)], axis=0)
            s = lax.dot_general(q2, kw, _NT, preferred_element_type=jnp.float32)
            halves = []
            for half in range(2):
                slope = 2.0 ** (-(2 * pair + half + 1))
                sh = s[half * ATTN_SUB:(half + 1) * ATTN_SUB] + slope * base
                m = jnp.max(sh, axis=-1, keepdims=True)
                halves.append((m, jnp.exp((sh - m).astype(jnp.bfloat16))))
            p2 = jnp.concatenate([halves[0][1], halves[1][1]], axis=0)
            vext = jnp.concatenate([vw, ones], axis=1)
            o2 = jnp.dot(p2, vext, preferred_element_type=jnp.float32)
            outs = []
            for half in range(2):
                rows = slice(half * ATTN_SUB, (half + 1) * ATTN_SUB)
                num = o2[rows, :128]
                den = o2[rows, 128:]
                outs.append(num / den)
                lse = halves[half][0] + jnp.log(den)
                head = 2 * pair + half
                lse_packed = lse if head == 0 else jnp.where(
                    lane >= head * LSE_LANES, lse, lse_packed)
            o_ref[0, r0:r0 + ATTN_SUB, lanes] = jnp.where(
                first_head, outs[0], outs[1]).astype(o_ref.dtype)
        lse_ref[0, r0:r0 + ATTN_SUB, :] = lse_packed


def _attention(qkv, col0, dilation):
    nsub, sub_len, _ = qkv.shape
    tq = next(t for t in (512, 256, ATTN_SUB) if sub_len % t == 0)
    halo_per_tile = tq // HALF_TAPS
    last_halo = sub_len // HALF_TAPS - 1

    def cur(which):
        return lambda si, i: (si, i, col0 + which)

    def prev(which):
        return lambda si, i: (si, jnp.maximum(i * halo_per_tile - 1, 0), col0 + which)

    def nxt(which):
        return lambda si, i: (si, jnp.minimum((i + 1) * halo_per_tile, last_halo), col0 + which)

    main = lambda which: pl.BlockSpec((1, tq, ATTN_GROUP_WIDTH), cur(which))
    halo = lambda fn, which: pl.BlockSpec((1, HALF_TAPS, ATTN_GROUP_WIDTH), fn(which))
    out = lambda width: pl.BlockSpec((1, tq, width), lambda si, i: (si, i, 0))
    return pl.pallas_call(
        functools.partial(_attn_kernel, dilation=dilation, sub_len=sub_len, tq=tq),
        grid=(nsub, sub_len // tq),
        in_specs=[main(0), halo(prev, 1), main(1), halo(nxt, 1), halo(prev, 2), main(2), halo(nxt, 2)],
        out_specs=[out(ATTN_GROUP_WIDTH), out(LANES)],
        out_shape=[jax.ShapeDtypeStruct((nsub, sub_len, ATTN_GROUP_WIDTH), jnp.bfloat16),
                   jax.ShapeDtypeStruct((nsub, sub_len, LANES), jnp.float32)],
        compiler_params=pltpu.CompilerParams(
            dimension_semantics=("arbitrary", "arbitrary"), vmem_limit_bytes=VMEM_LIMIT),
        name=f"attn_d{dilation}",
    )(qkv, qkv, qkv, qkv, qkv, qkv, qkv)


def _hgrn_tile_stages(rows, tri_ref, qs_ref, gate, v_ref, carry, o_ref, *, reverse, finish=None):
    f32, bf16 = jnp.float32, jnp.bfloat16
    lf_hi_ref, lf_mid_ref, k_ref = gate
    parts = jnp.dot(tri_ref[...],
                    jnp.concatenate([lf_hi_ref[0, rows, :], lf_mid_ref[0, rows, :]], axis=1),
                    preferred_element_type=f32)
    yield
    bcum = parts[:, HGRN_DIM:] + parts[:, :HGRN_DIM]
    mid, last = (HGRN_TILE // 2, 0) if reverse else (HGRN_TILE // 2 - 1, HGRN_TILE - 1)
    b_mid = bcum[mid:mid + 1]
    b_last = bcum[last:last + 1]
    decay = jnp.exp2(bcum - b_mid)
    q_t = (qs_ref[0, rows, :].astype(f32) * decay).astype(bf16)
    k_mid = k_ref[0, rows, :].astype(f32) / decay
    k_end = (k_mid * jnp.exp2(b_last - b_mid)).astype(bf16)
    v = v_ref[0, rows, :]
    a = lax.dot_general(q_t, k_mid.astype(bf16), _NT, preferred_element_type=f32)
    update = lax.dot_general(v, k_end, _TN, preferred_element_type=f32)
    yield
    state = carry["state"]
    inter = lax.dot_general(q_t, (state * jnp.exp2(b_mid)).astype(bf16), _NT,
                            preferred_element_type=f32)
    carry["state"] = state * jnp.exp2(b_last) + update
    tpos = lax.broadcasted_iota(jnp.int32, (HGRN_TILE, HGRN_TILE), 0)
    spos = lax.broadcasted_iota(jnp.int32, (HGRN_TILE, HGRN_TILE), 1)
    seen = (tpos <= spos) if reverse else (tpos >= spos)
    a = jnp.where(seen, a, 0.0).astype(bf16)
    intra = jnp.dot(a, v, preferred_element_type=f32)
    yield
    if finish is None:
        o_ref[rows, :] = intra + inter
    else:
        finish(rows, intra + inter + o_ref[rows, :])


def _hgrn_kernel(qs_ref, fhi_ref, fmid_ref, fk_ref, bhi_ref, bmid_ref, bk_ref, v_ref, hg_ref,
                 ng_ref, trif_ref, trib_ref, out_ref, of_ref, ob_ref, stf_ref, stb_ref, *, seq):
    n_tiles = seq // HGRN_TILE
    stf_ref[...] = jnp.zeros_like(stf_ref)
    stb_ref[...] = jnp.zeros_like(stb_ref)
    fwd_gate = (fhi_ref, fmid_ref, fk_ref)
    bwd_gate = (bhi_ref, bmid_ref, bk_ref)

    def finish(rows, o):
        o = _rms(o, ng_ref[...]) * hg_ref[0, rows, :].astype(jnp.float32)
        out_ref[0, rows, :] = o.astype(jnp.bfloat16)

    def step(t, loop_carry, *, second_half):
        fwd = {"state": stf_ref[...]}
        bwd = {"state": stb_ref[...]}
        tiles = []
        for j in range(HGRN_UNROLL):
            tf = t * HGRN_UNROLL + j
            fwd_rows = pl.ds(pl.multiple_of(tf * HGRN_TILE, HGRN_TILE), HGRN_TILE)
            bwd_rows = pl.ds(pl.multiple_of((n_tiles - 1 - tf) * HGRN_TILE, HGRN_TILE), HGRN_TILE)
            tiles.append(_hgrn_tile_stages(
                fwd_rows, trif_ref, qs_ref, fwd_gate, v_ref, fwd, ob_ref if second_half else of_ref,
                reverse=False, finish=finish if second_half else None))
            tiles.append(_hgrn_tile_stages(
                bwd_rows, trib_ref, qs_ref, bwd_gate, v_ref, bwd, of_ref if second_half else ob_ref,
                reverse=True, finish=finish if second_half else None))
        while tiles:
            tiles = [g for g in tiles if next(g, _DONE) is not _DONE]
        stf_ref[...] = fwd["state"]
        stb_ref[...] = bwd["state"]
        return loop_carry

    n_steps = n_tiles // HGRN_UNROLL
    lax.fori_loop(0, n_steps // 2, functools.partial(step, second_half=False), 0)
    lax.fori_loop(n_steps // 2, n_steps, functools.partial(step, second_half=True), 0)


def _tile_triangles():
    r = jnp.arange(HGRN_TILE)[:, None]
    c = jnp.arange(HGRN_TILE)[None, :]
    return (r >= c).astype(jnp.bfloat16), (r <= c).astype(jnp.bfloat16)


def _hgrn(p3d, norm_g):
    b, s, _ = p3d.shape
    assert s % (2 * HGRN_UNROLL * HGRN_TILE) == 0
    tri_f, tri_b = _tile_triangles()

    def head_cols(col):
        return pl.BlockSpec((1, s, HGRN_DIM), lambda bi, h: (bi, 0, col // HGRN_DIM + h))

    planes = [COL_QS] + [c0 + i * D_MODEL for c0 in (COL_FWD, COL_BWD) for i in range(3)] + [
        COL_HI, COL_HG]
    return pl.pallas_call(
        functools.partial(_hgrn_kernel, seq=s),
        grid=(b, HGRN_HEADS),
        in_specs=[head_cols(col) for col in planes] + [
            pl.BlockSpec((1, HGRN_DIM), lambda bi, h: (0, h)),
            _resident(tri_f.shape), _resident(tri_b.shape)],
        out_specs=pl.BlockSpec((1, s, HGRN_DIM), lambda bi, h: (bi, 0, h)),
        out_shape=jax.ShapeDtypeStruct((b, s, HGRN_HEADS * HGRN_DIM), jnp.bfloat16),
        scratch_shapes=[
            pltpu.VMEM((s, HGRN_DIM), jnp.float32),
            pltpu.VMEM((s, HGRN_DIM), jnp.float32),
            pltpu.VMEM((HGRN_DIM, HGRN_DIM), jnp.float32),
            pltpu.VMEM((HGRN_DIM, HGRN_DIM), jnp.float32),
        ],
        compiler_params=pltpu.CompilerParams(
            dimension_semantics=("arbitrary", "arbitrary"), vmem_limit_bytes=VMEM_LIMIT),
        name="hgrn2",
    )(*([p3d] * len(planes)), norm_g, tri_f, tri_b)


def _mix_kernel(x1_ref, gates_ref, o0_ref, l0_ref, o4_ref, l4_ref, o16_ref, l16_ref, yb_ref,
                widen_ref, wa_ref, wb_ref, wo_ref, x2_ref, so4_ref, sl4_ref, so16_ref, sl16_ref):
    f32, bf16 = jnp.float32, jnp.bfloat16
    for d, pairs in ((4, ((o4_ref, so4_ref), (l4_ref, sl4_ref))),
                     (16, ((o16_ref, so16_ref), (l16_ref, sl16_ref)))):
        n = MIX_ROWS // d
        for src_ref, dst_ref in pairs:
            for r in range(d):
                rows = src_ref[0, r].astype(f32)
                for c in range(dst_ref.shape[0]):
                    dst_ref[c, pl.ds(r, n, stride=d), :] = rows[:, c * LANES:(c + 1) * LANES]
    tokens = lambda ref: jnp.concatenate([ref[c] for c in range(ref.shape[0])], axis=1)
    l0, l1, l2 = l0_ref[0], sl4_ref[0], sl16_ref[0]
    m = jnp.maximum(jnp.maximum(l0, l1), l2)
    e0, e1, e2 = jnp.exp(l0 - m), jnp.exp(l1 - m), jnp.exp(l2 - m)
    inv = 1.0 / (e0 + e1 + e2)

    def widen(w):
        hi = w.astype(bf16)
        mid = (w - hi.astype(f32)).astype(bf16)
        return jnp.dot(jnp.concatenate([hi, mid], axis=1), widen_ref[...],
                       preferred_element_type=f32)

    ya = (widen(e0 * inv) * o0_ref[0].astype(f32) + widen(e1 * inv) * tokens(so4_ref)
          + widen(e2 * inv) * tokens(so16_ref))
    ma = jnp.dot(ya.astype(bf16), wa_ref[...], preferred_element_type=f32)
    mb = jnp.dot(yb_ref[0], wb_ref[...], preferred_element_type=f32)
    ga = gates_ref[0, :, :D_MODEL].astype(f32)
    gb = gates_ref[0, :, D_MODEL:].astype(f32)
    merged = jax.nn.sigmoid(ga) * ma + jax.nn.sigmoid(gb) * mb
    x2_ref[0] = x1_ref[0] + jnp.dot(merged.astype(bf16), wo_ref[...], preferred_element_type=f32)


def _widen_matrix():
    k = jnp.arange(2 * LANES)[:, None] % LANES
    head = jnp.arange(ATTN_GROUP_WIDTH)[None, :] // ATTN_HEAD_DIM
    return (k == head * LSE_LANES).astype(jnp.bfloat16)


def _mix(x1, pm, attn0, attn4, attn16, yb, wa, wb, wo):
    b, s, _ = x1.shape
    widen = _widen_matrix()
    tok = lambda width: pl.BlockSpec((1, MIX_ROWS, width), lambda bi, i: (bi, i, 0))
    sub = lambda d, width: pl.BlockSpec((1, d, MIX_ROWS // d, width), lambda bi, i: (bi, 0, i, 0))
    planes = lambda width: pltpu.VMEM((width // LANES, MIX_ROWS, LANES), jnp.float32)
    return pl.pallas_call(
        _mix_kernel,
        grid=(b, s // MIX_ROWS),
        in_specs=[tok(D_MODEL), tok(2 * D_MODEL), tok(ATTN_GROUP_WIDTH), tok(LANES),
                  sub(4, ATTN_GROUP_WIDTH), sub(4, LANES), sub(16, ATTN_GROUP_WIDTH), sub(16, LANES),
                  tok(D_MODEL), _resident(widen.shape),
                  _resident(wa.shape), _resident(wb.shape), _resident(wo.shape)],
        out_specs=tok(D_MODEL),
        out_shape=jax.ShapeDtypeStruct((b, s, D_MODEL), jnp.float32),
        scratch_shapes=[planes(ATTN_GROUP_WIDTH), planes(LANES)] * 2,
        compiler_params=pltpu.CompilerParams(
            dimension_semantics=("arbitrary", "arbitrary"), vmem_limit_bytes=VMEM_LIMIT),
        name="mix",
    )(x1, pm, *attn0, *attn4, *attn16, yb, widen, wa, wb, wo)


def _ffn_weights(w_gu, w_down):
    bf16 = jnp.bfloat16
    return w_gu[:, :D_FF].astype(bf16), w_gu[:, D_FF:].astype(bf16), w_down.astype(bf16)


def _in_weights(w_in):
    aw = 3 * ATTN_GROUP_WIDTH
    a_q, a_k, a_v = w_in[:, :aw], w_in[:, aw:2 * aw], w_in[:, 2 * aw:3 * aw]
    rest = w_in[:, 3 * aw:]
    h_q, h_ff, h_fb, h_i, h_g, g_a, g_b = jnp.split(rest, 7, axis=1)
    q_scale = 1.0 / math.sqrt(ATTN_HEAD_DIM)

    def qkv(g):
        sl = slice(g * ATTN_GROUP_WIDTH, (g + 1) * ATTN_GROUP_WIDTH)
        return [a_q[:, sl] * q_scale, a_k[:, sl], a_v[:, sl]]

    cat = lambda cols: jnp.concatenate(cols, axis=1).astype(jnp.bfloat16)
    return cat([g_a, g_b, h_q, h_ff, h_fb, h_i, h_g]), cat(qkv(0)), cat(qkv(1)), cat(qkv(2))


def _lower_bound(lb_raw, layer):
    return jnp.cumsum(jax.nn.softmax(lb_raw.astype(jnp.float32), axis=0), axis=0)[layer][None, :]


def _trunk(x, params):
    (g1, gm, ffn1_w, w_in, lb_f, lb_b, hgrn_norm, wa, wb, wo, g2, gf, ffn2_w) = params
    b, s, _ = x.shape
    n = b * s
    x1 = _ffn(x.reshape(n, D_MODEL), g1, ffn1_w).reshape(b, s, D_MODEL)
    pm = _proj_main(x1, gm, lb_f, lb_b, w_in[0])
    p1, p4, p16 = _proj_attn(x1, gm, *w_in[1:])
    attn0 = _attention(p1, 0, 1)
    dilated = []
    for d, qkv in ((4, p4), (16, p16)):
        o, lse = _attention(qkv.reshape(b * d, s // d, ATTN_QKV_WIDTH), 0, d)
        dilated.append((o.reshape(b, d, s // d, ATTN_GROUP_WIDTH),
                        lse.reshape(b, d, s // d, LANES)))
    yb = _hgrn(pm, hgrn_norm)
    x2 = _mix(x1, pm, attn0, *dilated, yb, wa, wb, wo)
    y = _ffn(x2.reshape(n, D_MODEL), g2, ffn2_w, gout=gf)
    return y.reshape(b, s, D_MODEL)


def kernel(x_prompt, x_sample, ffn1_norm, ffn1_w_gu, ffn1_w_down, mix_norm, w_in, hgrn_lb_fwd,
           hgrn_lb_bwd, hgrn_norm, w_branch_a, w_branch_b, w_out, ffn2_norm, ffn2_w_gu,
           ffn2_w_down, final_norm):
    bf16 = jnp.bfloat16
    layer = 0
    params = (
        ffn1_norm[layer][None, :], mix_norm[layer][None, :],
        _ffn_weights(ffn1_w_gu[layer], ffn1_w_down[layer]),
        _in_weights(w_in[layer]),
        _lower_bound(hgrn_lb_fwd, layer), _lower_bound(hgrn_lb_bwd, layer),
        hgrn_norm[layer][None, :],
        w_branch_a[layer].astype(bf16), w_branch_b[layer].astype(bf16), w_out[layer].astype(bf16),
        ffn2_norm[layer][None, :], final_norm[None, :],
        _ffn_weights(ffn2_w_gu[layer], ffn2_w_down[layer]),
    )
    return (_trunk(x_prompt, params), _trunk(x_sample, params))
```

```python
import functools
import math

import jax
import jax.numpy as jnp
from jax import lax
from jax.experimental import pallas as pl
from jax.experimental.pallas import tpu as pltpu

LANES = 128
D_MODEL = 1024
D_FF = 2816
RMS_EPS = 1e-6
MASK_BIAS = -1e30

ATTN_HEADS = 8
ATTN_HEAD_DIM = 64
ATTN_GROUP_WIDTH = ATTN_HEADS * ATTN_HEAD_DIM
HALF_TAPS = 64
LSE_LANES = LANES // ATTN_HEADS
ATTN_SUB = 128
ATTN_WIN = ATTN_SUB + 2 * HALF_TAPS

HGRN_HEADS = 8
HGRN_DIM = 128
HGRN_TILE = 128
HGRN_UNROLL = 8
_DONE = object()

FF_CHUNK = 256
N_FF_CHUNKS = D_FF // FF_CHUNK

FFN_ROWS = 1024
PROJ_ROWS = 512
MIX_ROWS = 1024

COL_GATES = 0
COL_QS = 2048
COL_FWD = 3072
COL_BWD = 6144
COL_HI = 9216
COL_HG = 10240
MAIN_WIDTH = COL_HG + D_MODEL
WCOL_GATES, WCOL_HQ, WCOL_HFF, WCOL_HFB, WCOL_HI, WCOL_HG = 0, 2048, 3072, 4096, 5120, 6144
ATTN_QKV_WIDTH = 3 * ATTN_GROUP_WIDTH
PROJ_CHUNK = 256

V7X_VMEM_BYTES = 64 * 1024 * 1024
VMEM_LIMIT = V7X_VMEM_BYTES - 8 * 1024 * 1024

_NT = (((1,), (1,)), ((), ()))
_TN = (((0,), (0,)), ((), ()))


def _resident(shape):
    zeros = (0,) * len(shape)
    return pl.BlockSpec(shape, lambda *_: zeros, pipeline_mode=pl.Buffered(1))


def _rms(x, g):
    return x * lax.rsqrt(jnp.mean(x * x, axis=-1, keepdims=True) + RMS_EPS) * g


def _swiglu_into(acc_ref, h_ref, wg_ref, wu_ref, wd_ref):
    f32 = jnp.float32

    def gate_up(c):
        cols = slice(c * FF_CHUNK, (c + 1) * FF_CHUNK)
        a = jnp.dot(h_ref[...], wg_ref[:, cols], preferred_element_type=f32)
        b = jnp.dot(h_ref[...], wu_ref[:, cols], preferred_element_type=f32)
        return (a * jax.nn.sigmoid(a) * b).astype(jnp.bfloat16)

    def down(c, act):
        part = jnp.dot(act, wd_ref[c * FF_CHUNK:(c + 1) * FF_CHUNK, :], preferred_element_type=f32)
        if c == 0:
            acc_ref[...] = part
        else:
            acc_ref[...] += part

    act = gate_up(0)
    for c in range(1, N_FF_CHUNKS):
        nxt = gate_up(c)
        down(c - 1, act)
        act = nxt
    down(N_FF_CHUNKS - 1, act)


def _ffn_kernel(x_ref, gin_ref, wg_ref, wu_ref, wd_ref, y_ref, hs_ref, acc_ref):
    x = x_ref[...]
    hs_ref[...] = _rms(x, gin_ref[...]).astype(jnp.bfloat16)
    _swiglu_into(acc_ref, hs_ref, wg_ref, wu_ref, wd_ref)
    y_ref[...] = x + 0.5 * acc_ref[...]


def _ffn_final_kernel(x_ref, gin_ref, gout_ref, wg_ref, wu_ref, wd_ref, y_ref, hs_ref, acc_ref):
    x = x_ref[...]
    hs_ref[...] = _rms(x, gin_ref[...]).astype(jnp.bfloat16)
    _swiglu_into(acc_ref, hs_ref, wg_ref, wu_ref, wd_ref)
    y_ref[...] = _rms(x + 0.5 * acc_ref[...], gout_ref[...])


def _ffn(x2d, gin, weights, gout=None):
    n = x2d.shape[0]
    row = pl.BlockSpec((FFN_ROWS, D_MODEL), lambda i: (i, 0))
    gains = [gin] if gout is None else [gin, gout]
    return pl.pallas_call(
        _ffn_kernel if gout is None else _ffn_final_kernel,
        grid=(n // FFN_ROWS,),
        in_specs=[row] + [_resident((1, D_MODEL))] * len(gains) + [_resident(w.shape) for w in weights],
        out_specs=row,
        out_shape=jax.ShapeDtypeStruct((n, D_MODEL), jnp.float32),
        scratch_shapes=[
            pltpu.VMEM((FFN_ROWS, D_MODEL), jnp.bfloat16),
            pltpu.VMEM((FFN_ROWS, D_MODEL), jnp.float32),
        ],
        compiler_params=pltpu.CompilerParams(
            dimension_semantics=("arbitrary",), vmem_limit_bytes=VMEM_LIMIT),
        name="ffn" if gout is None else "ffn_final",
    )(x2d, *gains, *weights)


def _proj_main_kernel(x1_ref, gm_ref, lbf_ref, lbb_ref, wm_ref, pm_ref, hb_ref):
    f32, bf16 = jnp.float32, jnp.bfloat16
    hb_ref[...] = _rms(x1_ref[0], gm_ref[...]).astype(bf16)

    def project(wcol):
        return jnp.dot(hb_ref[...], wm_ref[:, wcol:wcol + PROJ_CHUNK], preferred_element_type=f32)

    def put(col, val):
        pm_ref[0, :, col:col + PROJ_CHUNK] = val.astype(bf16)

    def plain(wcol, col):
        put(col, project(wcol))

    def silu(wcol, col):
        hq = project(wcol)
        put(col, hq * jax.nn.sigmoid(hq))

    def gate(wcol, col, lb):
        k = (1.0 - lb) * jax.nn.sigmoid(-project(wcol))
        log_f = jnp.log2(1.0 - k)
        hi = log_f.astype(bf16)
        put(col, hi)
        put(col + D_MODEL, log_f - hi.astype(f32))
        put(col + 2 * D_MODEL, k)

    chunks = lambda width: range(0, width, PROJ_CHUNK)
    gates = [functools.partial(gate, wcol0 + off, col0 + off, lb_ref[:, off:off + PROJ_CHUNK])
             for wcol0, col0, lb_ref in ((WCOL_HFF, COL_FWD, lbf_ref), (WCOL_HFB, COL_BWD, lbb_ref))
             for off in chunks(D_MODEL)]
    silus = [functools.partial(silu, wcol0 + off, col0 + off)
             for wcol0, col0 in ((WCOL_HQ, COL_QS), (WCOL_HG, COL_HG)) for off in chunks(D_MODEL)]
    plains = [functools.partial(plain, wcol0 + off, col0 + off)
              for wcol0, col0, width in ((WCOL_GATES, COL_GATES, 2 * D_MODEL),
                                         (WCOL_HI, COL_HI, D_MODEL))
              for off in chunks(width)]
    heavy = gates + silus
    for i, task in enumerate(heavy):
        task()
        if i < len(plains):
            plains[i]()
    for p in plains[len(heavy):]:
        p()


def _proj_main(x1, gm, lb_f, lb_b, w_main):
    b, s, _ = x1.shape
    tok = lambda width: pl.BlockSpec((1, PROJ_ROWS, width), lambda bi, i: (bi, i, 0))
    return pl.pallas_call(
        _proj_main_kernel,
        grid=(b, s // PROJ_ROWS),
        in_specs=[tok(D_MODEL), _resident((1, D_MODEL)), _resident((1, D_MODEL)),
                  _resident((1, D_MODEL)), _resident(w_main.shape)],
        out_specs=tok(MAIN_WIDTH),
        out_shape=jax.ShapeDtypeStruct((b, s, MAIN_WIDTH), jnp.bfloat16),
        scratch_shapes=[pltpu.VMEM((PROJ_ROWS, D_MODEL), jnp.bfloat16)],
        compiler_params=pltpu.CompilerParams(
            dimension_semantics=("arbitrary", "arbitrary"), vmem_limit_bytes=VMEM_LIMIT),
        name="proj_main",
    )(x1, gm, lb_f, lb_b, w_main)


def _proj_attn_kernel(x1_ref, gm_ref, w1_ref, w4_ref, w16_ref, p1_ref, p4_ref, p16_ref,
                      hs_ref, hb_ref):
    f32, bf16 = jnp.float32, jnp.bfloat16
    hf = _rms(x1_ref[0], gm_ref[...])
    for c in range(D_MODEL // LANES):
        hs_ref[c] = hf[:, c * LANES:(c + 1) * LANES]
    hb_ref[0] = hf.astype(bf16)
    for slot, d in ((1, 4), (2, 16)):
        n = PROJ_ROWS // d
        hb_ref[slot] = jnp.concatenate(
            [jnp.concatenate([hs_ref[c, pl.ds(r, n, stride=d), :]
                              for c in range(D_MODEL // LANES)], axis=1)
             for r in range(d)], axis=0).astype(bf16)
    for j in range(ATTN_QKV_WIDTH // PROJ_CHUNK):
        cols = slice(j * PROJ_CHUNK, (j + 1) * PROJ_CHUNK)
        p1_ref[0, :, cols] = jnp.dot(
            hb_ref[0], w1_ref[:, cols], preferred_element_type=f32).astype(bf16)
    for slot, d, w_ref, p_ref in ((1, 4, w4_ref, p4_ref), (2, 16, w16_ref, p16_ref)):
        n = PROJ_ROWS // d
        for j in range(ATTN_QKV_WIDTH // PROJ_CHUNK):
            cols = slice(j * PROJ_CHUNK, (j + 1) * PROJ_CHUNK)
            res = jnp.dot(hb_ref[slot], w_ref[:, cols], preferred_element_type=f32).astype(bf16)
            for r in range(d):
                p_ref[0, r, :, cols] = res[r * n:(r + 1) * n]


def _proj_attn(x1, gm, w_d1, w_d4, w_d16):
    b, s, _ = x1.shape
    bf16 = jnp.bfloat16

    def dilated(d):
        spec = pl.BlockSpec((1, d, PROJ_ROWS // d, ATTN_QKV_WIDTH), lambda bi, i: (bi, 0, i, 0))
        return spec, jax.ShapeDtypeStruct((b, d, s // d, ATTN_QKV_WIDTH), bf16)

    (spec4, sds4), (spec16, sds16) = dilated(4), dilated(16)
    tok = lambda width: pl.BlockSpec((1, PROJ_ROWS, width), lambda bi, i: (bi, i, 0))
    return pl.pallas_call(
        _proj_attn_kernel,
        grid=(b, s // PROJ_ROWS),
        in_specs=[tok(D_MODEL), _resident((1, D_MODEL)),
                  _resident(w_d1.shape), _resident(w_d4.shape), _resident(w_d16.shape)],
        out_specs=[tok(ATTN_QKV_WIDTH), spec4, spec16],
        out_shape=[jax.ShapeDtypeStruct((b, s, ATTN_QKV_WIDTH), bf16), sds4, sds16],
        scratch_shapes=[pltpu.VMEM((D_MODEL // LANES, PROJ_ROWS, LANES), jnp.float32),
                        pltpu.VMEM((3, PROJ_ROWS, D_MODEL), jnp.bfloat16)],
        compiler_params=pltpu.CompilerParams(
            dimension_semantics=("arbitrary", "arbitrary"), vmem_limit_bytes=VMEM_LIMIT),
        name="proj_attn",
    )(x1, gm, w_d1, w_d4, w_d16)


def _attn_kernel(q_ref, kp_ref, kc_ref, kn_ref, vp_ref, vc_ref, vn_ref, o_ref, lse_ref,
                 *, dilation, sub_len, tq):
    q0 = pl.program_id(1) * tq
    kwin = jnp.concatenate([kp_ref[0], kc_ref[0], kn_ref[0]], axis=0)
    vwin = jnp.concatenate([vp_ref[0], vc_ref[0], vn_ref[0]], axis=0)

    row = lax.broadcasted_iota(jnp.int32, (ATTN_SUB, ATTN_WIN), 0)
    col = lax.broadcasted_iota(jnp.int32, (ATTN_SUB, ATTN_WIN), 1)
    rel = jnp.abs(col - HALF_TAPS - row)
    lane = lax.broadcasted_iota(jnp.int32, (ATTN_SUB, 2 * ATTN_HEAD_DIM), 1)
    first_head = lane < ATTN_HEAD_DIM
    ones = jnp.ones((ATTN_WIN, 2 * ATTN_HEAD_DIM), jnp.bfloat16)

    for sub in range(tq // ATTN_SUB):
        r0 = sub * ATTN_SUB
        kpos = q0 + (r0 - HALF_TAPS) + col
        valid = (rel <= HALF_TAPS) & (kpos >= 0) & (kpos < sub_len)
        base = jnp.where(valid, (-dilation) * rel.astype(jnp.float32), MASK_BIAS)
        lse_packed = None
        for pair in range(ATTN_HEADS // 2):
            lanes = slice(pair * LANES, (pair + 1) * LANES)
            qp = q_ref[0, r0:r0 + ATTN_SUB, lanes]
            kw = kwin[r0:r0 + ATTN_WIN, lanes]
            vw = vwin[r0:r0 + ATTN_WIN, lanes]
            zero = jnp.zeros_like(qp)
            q2 = jnp.concatenate(
                [jnp.where(first_head, qp, zero), jnp.where(first_head, zero, qp)], axis=0)
            s = lax.dot_general(q2, kw, _NT, preferred_element_type=jnp.float32)
            halves = []
            for half in range(2):
                slope = 2.0 ** (-(2 * pair + half + 1))
                sh = s[half * ATTN_SUB:(half + 1) * ATTN_SUB] + slope * base
                m = jnp.max(sh, axis=-1, keepdims=True)
                halves.append((m, jnp.exp((sh - m).astype(jnp.bfloat16))))
            p2 = jnp.concatenate([halves[0][1], halves[1][1]], axis=0)
            vext = jnp.concatenate([vw, ones], axis=1)
            o2 = jnp.dot(p2, vext, preferred_element_type=jnp.float32)
            outs = []
            for half in range(2):
                rows = slice(half * ATTN_SUB, (half + 1) * ATTN_SUB)
                num = o2[rows, :LANES]
                den = o2[rows, LANES:]
                outs.append(num / den)
                lse = halves[half][0] + jnp.log(den)
                head = 2 * pair + half
                lse_packed = lse if head == 0 else jnp.where(
                    lane >= head * LSE_LANES, lse, lse_packed)
            o_ref[0, r0:r0 + ATTN_SUB, lanes] = jnp.where(
                first_head, outs[0], outs[1]).astype(o_ref.dtype)
        lse_ref[0, r0:r0 + ATTN_SUB, :] = lse_packed


def _attention(qkv, col0, dilation):
    nsub, sub_len, _ = qkv.shape
    tq = next(t for t in (512, 256, ATTN_SUB) if sub_len % t == 0)
    halo_per_tile = tq // HALF_TAPS
    last_halo = sub_len // HALF_TAPS - 1

    def cur(which):
        return lambda si, i: (si, i, col0 + which)

    def prev(which):
        return lambda si, i: (si, jnp.maximum(i * halo_per_tile - 1, 0), col0 + which)

    def nxt(which):
        return lambda si, i: (si, jnp.minimum((i + 1) * halo_per_tile, last_halo), col0 + which)

    main = lambda which: pl.BlockSpec((1, tq, ATTN_GROUP_WIDTH), cur(which))
    halo = lambda fn, which: pl.BlockSpec((1, HALF_TAPS, ATTN_GROUP_WIDTH), fn(which))
    out = lambda width: pl.BlockSpec((1, tq, width), lambda si, i: (si, i, 0))
    return pl.pallas_call(
        functools.partial(_attn_kernel, dilation=dilation, sub_len=sub_len, tq=tq),
        grid=(nsub, sub_len // tq),
        in_specs=[main(0), halo(prev, 1), main(1), halo(nxt, 1), halo(prev, 2), main(2), halo(nxt, 2)],
        out_specs=[out(ATTN_GROUP_WIDTH), out(LANES)],
        out_shape=[jax.ShapeDtypeStruct((nsub, sub_len, ATTN_GROUP_WIDTH), jnp.bfloat16),
                   jax.ShapeDtypeStruct((nsub, sub_len, LANES), jnp.float32)],
        compiler_params=pltpu.CompilerParams(
            dimension_semantics=("arbitrary", "arbitrary"), vmem_limit_bytes=VMEM_LIMIT),
        name=f"attn_d{dilation}",
    )(qkv, qkv, qkv, qkv, qkv, qkv, qkv)


def _hgrn_tile_stages(rows, tri_ref, qs_ref, gate, v_ref, carry, o_ref, *, reverse, finish=None):
    f32, bf16 = jnp.float32, jnp.bfloat16
    lf_hi_ref, lf_mid_ref, k_ref = gate
    parts = jnp.dot(tri_ref[...],
                    jnp.concatenate([lf_hi_ref[0, rows, :], lf_mid_ref[0, rows, :]], axis=1),
                    preferred_element_type=f32)
    yield
    bcum = parts[:, HGRN_DIM:] + parts[:, :HGRN_DIM]
    mid, last = (HGRN_TILE // 2, 0) if reverse else (HGRN_TILE // 2 - 1, HGRN_TILE - 1)
    b_mid = bcum[mid:mid + 1]
    b_last = bcum[last:last + 1]
    decay = jnp.exp2(bcum - b_mid)
    q_t = (qs_ref[0, rows, :].astype(f32) * decay).astype(bf16)
    k_mid = k_ref[0, rows, :].astype(f32) / decay
    k_end = (k_mid * jnp.exp2(b_last - b_mid)).astype(bf16)
    v = v_ref[0, rows, :]
    a = lax.dot_general(q_t, k_mid.astype(bf16), _NT, preferred_element_type=f32)
    update = lax.dot_general(v, k_end, _TN, preferred_element_type=f32)
    yield
    state = carry["state"]
    inter = lax.dot_general(q_t, (state * jnp.exp2(b_mid)).astype(bf16), _NT,
                            preferred_element_type=f32)
    carry["state"] = state * jnp.exp2(b_last) + update
    tpos = lax.broadcasted_iota(jnp.int32, (HGRN_TILE, HGRN_TILE), 0)
    spos = lax.broadcasted_iota(jnp.int32, (HGRN_TILE, HGRN_TILE), 1)
    seen = (tpos <= spos) if reverse else (tpos >= spos)
    a = jnp.where(seen, a, 0.0).astype(bf16)
    intra = jnp.dot(a, v, preferred_element_type=f32)
    yield
    if finish is None:
        o_ref[rows, :] = intra + inter
    else:
        finish(rows, intra + inter + o_ref[rows, :])


def _hgrn_kernel(qs_ref, fhi_ref, fmid_ref, fk_ref, bhi_ref, bmid_ref, bk_ref, v_ref, hg_ref,
                 ng_ref, trif_ref, trib_ref, out_ref, of_ref, ob_ref, stf_ref, stb_ref, *, seq):
    n_tiles = seq // HGRN_TILE
    stf_ref[...] = jnp.zeros_like(stf_ref)
    stb_ref[...] = jnp.zeros_like(stb_ref)
    fwd_gate = (fhi_ref, fmid_ref, fk_ref)
    bwd_gate = (bhi_ref, bmid_ref, bk_ref)

    def finish(rows, o):
        o = _rms(o, ng_ref[...]) * hg_ref[0, rows, :].astype(jnp.float32)
        out_ref[0, rows, :] = o.astype(jnp.bfloat16)

    def step(t, loop_carry, *, second_half):
        fwd = {"state": stf_ref[...]}
        bwd = {"state": stb_ref[...]}
        tiles = []
        for j in range(HGRN_UNROLL):
            tf = t * HGRN_UNROLL + j
            fwd_rows = pl.ds(pl.multiple_of(tf * HGRN_TILE, HGRN_TILE), HGRN_TILE)
            bwd_rows = pl.ds(pl.multiple_of((n_tiles - 1 - tf) * HGRN_TILE, HGRN_TILE), HGRN_TILE)
            tiles.append(_hgrn_tile_stages(
                fwd_rows, trif_ref, qs_ref, fwd_gate, v_ref, fwd, ob_ref if second_half else of_ref,
                reverse=False, finish=finish if second_half else None))
            tiles.append(_hgrn_tile_stages(
                bwd_rows, trib_ref, qs_ref, bwd_gate, v_ref, bwd, of_ref if second_half else ob_ref,
                reverse=True, finish=finish if second_half else None))
        while tiles:
            tiles = [g for g in tiles if next(g, _DONE) is not _DONE]
        stf_ref[...] = fwd["state"]
        stb_ref[...] = bwd["state"]
        return loop_carry

    n_steps = n_tiles // HGRN_UNROLL
    lax.fori_loop(0, n_steps // 2, functools.partial(step, second_half=False), 0)
    lax.fori_loop(n_steps // 2, n_steps, functools.partial(step, second_half=True), 0)


def _tile_triangles():
    r = jnp.arange(HGRN_TILE)[:, None]
    c = jnp.arange(HGRN_TILE)[None, :]
    return (r >= c).astype(jnp.bfloat16), (r <= c).astype(jnp.bfloat16)


def _hgrn(p3d, norm_g):
    b, s, _ = p3d.shape
    assert s % (2 * HGRN_UNROLL * HGRN_TILE) == 0
    tri_f, tri_b = _tile_triangles()

    def head_cols(col):
        return pl.BlockSpec((1, s, HGRN_DIM), lambda bi, h: (bi, 0, col // HGRN_DIM + h))

    planes = [COL_QS] + [c0 + i * D_MODEL for c0 in (COL_FWD, COL_BWD) for i in range(3)] + [
        COL_HI, COL_HG]
    return pl.pallas_call(
        functools.partial(_hgrn_kernel, seq=s),
        grid=(b, HGRN_HEADS),
        in_specs=[head_cols(col) for col in planes] + [
            pl.BlockSpec((1, HGRN_DIM), lambda bi, h: (0, h)),
            _resident(tri_f.shape), _resident(tri_b.shape)],
        out_specs=pl.BlockSpec((1, s, HGRN_DIM), lambda bi, h: (bi, 0, h)),
        out_shape=jax.ShapeDtypeStruct((b, s, HGRN_HEADS * HGRN_DIM), jnp.bfloat16),
        scratch_shapes=[
            pltpu.VMEM((s, HGRN_DIM), jnp.float32),
            pltpu.VMEM((s, HGRN_DIM), jnp.float32),
            pltpu.VMEM((HGRN_DIM, HGRN_DIM), jnp.float32),
            pltpu.VMEM((HGRN_DIM, HGRN_DIM), jnp.float32),
        ],
        compiler_params=pltpu.CompilerParams(
            dimension_semantics=("arbitrary", "arbitrary"), vmem_limit_bytes=VMEM_LIMIT),
        name="hgrn2",
    )(*([p3d] * len(planes)), norm_g, tri_f, tri_b)


def _mix_kernel(x1_ref, gates_ref, o0_ref, l0_ref, o4_ref, l4_ref, o16_ref, l16_ref, yb_ref,
                widen_ref, wa_ref, wb_ref, wo_ref, x2_ref, so4_ref, sl4_ref, so16_ref, sl16_ref):
    f32, bf16 = jnp.float32, jnp.bfloat16
    for d, pairs in ((4, ((o4_ref, so4_ref), (l4_ref, sl4_ref))),
                     (16, ((o16_ref, so16_ref), (l16_ref, sl16_ref)))):
        n = MIX_ROWS // d
        for src_ref, dst_ref in pairs:
            for r in range(d):
                rows = src_ref[0, r].astype(f32)
                for c in range(dst_ref.shape[0]):
                    dst_ref[c, pl.ds(r, n, stride=d), :] = rows[:, c * LANES:(c + 1) * LANES]
    tokens = lambda ref: jnp.concatenate([ref[c] for c in range(ref.shape[0])], axis=1)
    l0, l1, l2 = l0_ref[0], sl4_ref[0], sl16_ref[0]
    m = jnp.maximum(jnp.maximum(l0, l1), l2)
    e0, e1, e2 = jnp.exp(l0 - m), jnp.exp(l1 - m), jnp.exp(l2 - m)
    inv = 1.0 / (e0 + e1 + e2)

    def widen(w):
        hi = w.astype(bf16)
        mid = (w - hi.astype(f32)).astype(bf16)
        return jnp.dot(jnp.concatenate([hi, mid], axis=1), widen_ref[...],
                       preferred_element_type=f32)

    ya = (widen(e0 * inv) * o0_ref[0].astype(f32) + widen(e1 * inv) * tokens(so4_ref)
          + widen(e2 * inv) * tokens(so16_ref))
    ma = jnp.dot(ya.astype(bf16), wa_ref[...], preferred_element_type=f32)
    mb = jnp.dot(yb_ref[0], wb_ref[...], preferred_element_type=f32)
    ga = gates_ref[0, :, :D_MODEL].astype(f32)
    gb = gates_ref[0, :, D_MODEL:].astype(f32)
    merged = jax.nn.sigmoid(ga) * ma + jax.nn.sigmoid(gb) * mb
    x2_ref[0] = x1_ref[0] + jnp.dot(merged.astype(bf16), wo_ref[...], preferred_element_type=f32)


def _widen_matrix():
    k = jnp.arange(2 * LANES)[:, None] % LANES
    head = jnp.arange(ATTN_GROUP_WIDTH)[None, :] // ATTN_HEAD_DIM
    return (k == head * LSE_LANES).astype(jnp.bfloat16)


def _mix(x1, pm, attn0, attn4, attn16, yb, wa, wb, wo):
    b, s, _ = x1.shape
    widen = _widen_matrix()
    tok = lambda width: pl.BlockSpec((1, MIX_ROWS, width), lambda bi, i: (bi, i, 0))
    sub = lambda d, width: pl.BlockSpec((1, d, MIX_ROWS // d, width), lambda bi, i: (bi, 0, i, 0))
    planes = lambda width: pltpu.VMEM((width // LANES, MIX_ROWS, LANES), jnp.float32)
    return pl.pallas_call(
        _mix_kernel,
        grid=(b, s // MIX_ROWS),
        in_specs=[tok(D_MODEL), tok(2 * D_MODEL), tok(ATTN_GROUP_WIDTH), tok(LANES),
                  sub(4, ATTN_GROUP_WIDTH), sub(4, LANES), sub(16, ATTN_GROUP_WIDTH), sub(16, LANES),
                  tok(D_MODEL), _resident(widen.shape),
                  _resident(wa.shape), _resident(wb.shape), _resident(wo.shape)],
        out_specs=tok(D_MODEL),
        out_shape=jax.ShapeDtypeStruct((b, s, D_MODEL), jnp.float32),
        scratch_shapes=[planes(ATTN_GROUP_WIDTH), planes(LANES)] * 2,
        compiler_params=pltpu.CompilerParams(
            dimension_semantics=("arbitrary", "arbitrary"), vmem_limit_bytes=VMEM_LIMIT),
        name="mix",
    )(x1, pm, *attn0, *attn4, *attn16, yb, widen, wa, wb, wo)


def _ffn_weights(w_gu, w_down):
    bf16 = jnp.bfloat16
    return w_gu[:, :D_FF].astype(bf16), w_gu[:, D_FF:].astype(bf16), w_down.astype(bf16)


def _in_weights(w_in):
    aw = 3 * ATTN_GROUP_WIDTH
    a_q, a_k, a_v = w_in[:, :aw], w_in[:, aw:2 * aw], w_in[:, 2 * aw:3 * aw]
    rest = w_in[:, 3 * aw:]
    h_q, h_ff, h_fb, h_i, h_g, g_a, g_b = jnp.split(rest, 7, axis=1)
    q_scale = 1.0 / math.sqrt(ATTN_HEAD_DIM)

    def qkv(g):
        sl = slice(g * ATTN_GROUP_WIDTH, (g + 1) * ATTN_GROUP_WIDTH)
        return [a_q[:, sl] * q_scale, a_k[:, sl], a_v[:, sl]]

    cat = lambda cols: jnp.concatenate(cols, axis=1).astype(jnp.bfloat16)
    return cat([g_a, g_b, h_q, h_ff, h_fb, h_i, h_g]), cat(qkv(0)), cat(qkv(1)), cat(qkv(2))


def _lower_bound(lb_raw, layer):
    return jnp.cumsum(jax.nn.softmax(lb_raw.astype(jnp.float32), axis=0), axis=0)[layer][None, :]


def _trunk(x, params):
    (g1, gm, ffn1_w, w_in, lb_f, lb_b, hgrn_norm, wa, wb, wo, g2, gf, ffn2_w) = params
    b, s, _ = x.shape
    n = b * s
    x1 = _ffn(x.reshape(n, D_MODEL), g1, ffn1_w).reshape(b, s, D_MODEL)
    pm = _proj_main(x1, gm, lb_f, lb_b, w_in[0])
    p1, p4, p16 = _proj_attn(x1, gm, *w_in[1:])
    attn0 = _attention(p1, 0, 1)
    dilated = []
    for d, qkv in ((4, p4), (16, p16)):
        o, lse = _attention(qkv.reshape(b * d, s // d, ATTN_QKV_WIDTH), 0, d)
        dilated.append((o.reshape(b, d, s // d, ATTN_GROUP_WIDTH),
                        lse.reshape(b, d, s // d, LANES)))
    yb = _hgrn(pm, hgrn_norm)
    x2 = _mix(x1, pm, attn0, *dilated, yb, wa, wb, wo)
    y = _ffn(x2.reshape(n, D_MODEL), g2, ffn2_w, gout=gf)
    return y.reshape(b, s, D_MODEL)


def kernel(x_prompt, x_sample, ffn1_norm, ffn1_w_gu, ffn1_w_down, mix_norm, w_in, hgrn_lb_fwd,
           hgrn_lb_bwd, hgrn_norm, w_branch_a, w_branch_b, w_out, ffn2_norm, ffn2_w_gu,
           ffn2_w_down, final_norm):
    bf16 = jnp.bfloat16
    layer = 0
    params = (
        ffn1_norm[layer][None, :], mix_norm[layer][None, :],
        _ffn_weights(ffn1_w_gu[layer], ffn1_w_down[layer]),
        _in_weights(w_in[layer]),
        _lower_bound(hgrn_lb_fwd, layer), _lower_bound(hgrn_lb_bwd, layer),
        hgrn_norm[layer][None, :],
        w_branch_a[layer].astype(bf16), w_branch_b[layer].astype(bf16), w_out[layer].astype(bf16),
        ffn2_norm[layer][None, :], final_norm[None, :],
        _ffn_weights(ffn2_w_gu[layer], ffn2_w_down[layer]),
    )
    return (_trunk(x_prompt, params), _trunk(x_sample, params))
```

```python
import functools
import math

import jax
import jax.numpy as jnp
from jax import lax
from jax.experimental import pallas as pl
from jax.experimental.pallas import tpu as pltpu

LANES = 128
D_MODEL = 1024
D_FF = 2816
RMS_EPS = 1e-6
MASK_BIAS = -1e30

ATTN_HEADS = 8
ATTN_HEAD_DIM = 64
ATTN_GROUP_WIDTH = ATTN_HEADS * ATTN_HEAD_DIM
HALF_TAPS = 64
LSE_LANES = LANES // ATTN_HEADS
ATTN_SUB = 128
ATTN_WIN = ATTN_SUB + 2 * HALF_TAPS

HGRN_HEADS = 8
HGRN_DIM = 128
HGRN_TILE = 128
HGRN_UNROLL = 8
_DONE = object()

FF_CHUNK = 256
N_FF_CHUNKS = D_FF // FF_CHUNK

FFN_ROWS = 1024
PROJ_ROWS = 512
MIX_ROWS = 1024

COL_GATES = 0
COL_QS = 2048
COL_FWD = 3072
COL_BWD = 6144
COL_HI = 9216
COL_HG = 10240
MAIN_WIDTH = COL_HG + D_MODEL
WCOL_GATES, WCOL_HQ, WCOL_HFF, WCOL_HFB, WCOL_HI, WCOL_HG = 0, 2048, 3072, 4096, 5120, 6144
ATTN_QKV_WIDTH = 3 * ATTN_GROUP_WIDTH
PROJ_CHUNK = 256

V7X_VMEM_BYTES = 64 * 1024 * 1024
VMEM_LIMIT = V7X_VMEM_BYTES - 8 * 1024 * 1024

_NT = (((1,), (1,)), ((), ()))
_TN = (((0,), (0,)), ((), ()))


def _resident(shape):
    zeros = (0,) * len(shape)
    return pl.BlockSpec(shape, lambda *_: zeros, pipeline_mode=pl.Buffered(1))


def _rms(x, g):
    return x * lax.rsqrt(jnp.mean(x * x, axis=-1, keepdims=True) + RMS_EPS) * g


def _swiglu_into(acc_ref, h_ref, wg_ref, wu_ref, wd_ref):
    f32 = jnp.float32

    def gate_up(c):
        cols = slice(c * FF_CHUNK, (c + 1) * FF_CHUNK)
        a = jnp.dot(h_ref[...], wg_ref[:, cols], preferred_element_type=f32)
        b = jnp.dot(h_ref[...], wu_ref[:, cols], preferred_element_type=f32)
        return (a * jax.nn.sigmoid(a) * b).astype(jnp.bfloat16)

    def down(c, act):
        part = jnp.dot(act, wd_ref[c * FF_CHUNK:(c + 1) * FF_CHUNK, :], preferred_element_type=f32)
        if c == 0:
            acc_ref[...] = part
        else:
            acc_ref[...] += part

    act = gate_up(0)
    for c in range(1, N_FF_CHUNKS):
        nxt = gate_up(c)
        down(c - 1, act)
        act = nxt
    down(N_FF_CHUNKS - 1, act)


def _ffn_kernel(x_ref, gin_ref, wg_ref, wu_ref, wd_ref, y_ref, hs_ref, acc_ref):
    x = x_ref[...]
    hs_ref[...] = _rms(x, gin_ref[...]).astype(jnp.bfloat16)
    _swiglu_into(acc_ref, hs_ref, wg_ref, wu_ref, wd_ref)
    y_ref[...] = x + 0.5 * acc_ref[...]


def _ffn_final_kernel(x_ref, gin_ref, gout_ref, wg_ref, wu_ref, wd_ref, y_ref, hs_ref, acc_ref):
    x = x_ref[...]
    hs_ref[...] = _rms(x, gin_ref[...]).astype(jnp.bfloat16)
    _swiglu_into(acc_ref, hs_ref, wg_ref, wu_ref, wd_ref)
    y_ref[...] = _rms(x + 0.5 * acc_ref[...], gout_ref[...])


def _ffn(x2d, gin, weights, gout=None):
    n = x2d.shape[0]
    row = pl.BlockSpec((FFN_ROWS, D_MODEL), lambda i: (i, 0))
    gains = [gin] if gout is None else [gin, gout]
    return pl.pallas_call(
        _ffn_kernel if gout is None else _ffn_final_kernel,
        grid=(n // FFN_ROWS,),
        in_specs=[row] + [_resident((1, D_MODEL))] * len(gains) + [_resident(w.shape) for w in weights],
        out_specs=row,
        out_shape=jax.ShapeDtypeStruct((n, D_MODEL), jnp.float32),
        scratch_shapes=[
            pltpu.VMEM((FFN_ROWS, D_MODEL), jnp.bfloat16),
            pltpu.VMEM((FFN_ROWS, D_MODEL), jnp.float32),
        ],
        compiler_params=pltpu.CompilerParams(
            dimension_semantics=("arbitrary",), vmem_limit_bytes=VMEM_LIMIT),
        name="ffn" if gout is None else "ffn_final",
    )(x2d, *gains, *weights)


def _proj_main_kernel(x1_ref, gm_ref, lbf_ref, lbb_ref, wm_ref, pm_ref, hb_ref):
    f32, bf16 = jnp.float32, jnp.bfloat16
    hb_ref[...] = _rms(x1_ref[0], gm_ref[...]).astype(bf16)

    def project(wcol):
        return jnp.dot(hb_ref[...], wm_ref[:, wcol:wcol + PROJ_CHUNK], preferred_element_type=f32)

    def put(col, val):
        pm_ref[0, :, col:col + PROJ_CHUNK] = val.astype(bf16)

    def plain(wcol, col):
        put(col, project(wcol))

    def silu(wcol, col):
        hq = project(wcol)
        put(col, hq * jax.nn.sigmoid(hq))

    def gate(wcol, col, lb):
        k = (1.0 - lb) * jax.nn.sigmoid(-project(wcol))
        log_f = jnp.log2(1.0 - k)
        hi = log_f.astype(bf16)
        put(col, hi)
        put(col + D_MODEL, log_f - hi.astype(f32))
        put(col + 2 * D_MODEL, k)

    chunks = lambda width: range(0, width, PROJ_CHUNK)
    gates = [functools.partial(gate, wcol0 + off, col0 + off, lb_ref[:, off:off + PROJ_CHUNK])
             for wcol0, col0, lb_ref in ((WCOL_HFF, COL_FWD, lbf_ref), (WCOL_HFB, COL_BWD, lbb_ref))
             for off in chunks(D_MODEL)]
    silus = [functools.partial(silu, wcol0 + off, col0 + off)
             for wcol0, col0 in ((WCOL_HQ, COL_QS), (WCOL_HG, COL_HG)) for off in chunks(D_MODEL)]
    plains = [functools.partial(plain, wcol0 + off, col0 + off)
              for wcol0, col0, width in ((WCOL_GATES, COL_GATES, 2 * D_MODEL),
                                         (WCOL_HI, COL_HI, D_MODEL))
              for off in chunks(width)]
    heavy = gates + silus
    for i, task in enumerate(heavy):
        task()
        if i < len(plains):
            plains[i]()
    for p in plains[len(heavy):]:
        p()


def _proj_main(x1, gm, lb_f, lb_b, w_main):
    b, s, _ = x1.shape
    tok = lambda width: pl.BlockSpec((1, PROJ_ROWS, width), lambda bi, i: (bi, i, 0))
    return pl.pallas_call(
        _proj_main_kernel,
        grid=(b, s // PROJ_ROWS),
        in_specs=[tok(D_MODEL), _resident((1, D_MODEL)), _resident((1, D_MODEL)),
                  _resident((1, D_MODEL)), _resident(w_main.shape)],
        out_specs=tok(MAIN_WIDTH),
        out_shape=jax.ShapeDtypeStruct((b, s, MAIN_WIDTH), jnp.bfloat16),
        scratch_shapes=[pltpu.VMEM((PROJ_ROWS, D_MODEL), jnp.bfloat16)],
        compiler_params=pltpu.CompilerParams(
            dimension_semantics=("arbitrary", "arbitrary"), vmem_limit_bytes=VMEM_LIMIT),
        name="proj_main",
    )(x1, gm, lb_f, lb_b, w_main)


def _proj_attn_kernel(x1_ref, gm_ref, w1_ref, w4_ref, w16_ref, p1_ref, p4_ref, p16_ref,
                      hs_ref, hb_ref):
    f32, bf16 = jnp.float32, jnp.bfloat16
    hf = _rms(x1_ref[0], gm_ref[...])
    for c in range(D_MODEL // LANES):
        hs_ref[c] = hf[:, c * LANES:(c + 1) * LANES]
    hb_ref[0] = hf.astype(bf16)
    for slot, d in ((1, 4), (2, 16)):
        n = PROJ_ROWS // d
        hb_ref[slot] = jnp.concatenate(
            [jnp.concatenate([hs_ref[c, pl.ds(r, n, stride=d), :]
                              for c in range(D_MODEL // LANES)], axis=1)
             for r in range(d)], axis=0).astype(bf16)
    for j in range(ATTN_QKV_WIDTH // PROJ_CHUNK):
        cols = slice(j * PROJ_CHUNK, (j + 1) * PROJ_CHUNK)
        p1_ref[0, :, cols] = jnp.dot(
            hb_ref[0], w1_ref[:, cols], preferred_element_type=f32).astype(bf16)
    for slot, d, w_ref, p_ref in ((1, 4, w4_ref, p4_ref), (2, 16, w16_ref, p16_ref)):
        n = PROJ_ROWS // d
        for j in range(ATTN_QKV_WIDTH // PROJ_CHUNK):
            cols = slice(j * PROJ_CHUNK, (j + 1) * PROJ_CHUNK)
            res = jnp.dot(hb_ref[slot], w_ref[:, cols], preferred_element_type=f32).astype(bf16)
            for r in range(d):
                p_ref[0, r, :, cols] = res[r * n:(r + 1) * n]


def _proj_attn(x1, gm, w_d1, w_d4, w_d16):
    b, s, _ = x1.shape
    bf16 = jnp.bfloat16

    def dilated(d):
        spec = pl.BlockSpec((1, d, PROJ_ROWS // d, ATTN_QKV_WIDTH), lambda bi, i: (bi, 0, i, 0))
        return spec, jax.ShapeDtypeStruct((b, d, s // d, ATTN_QKV_WIDTH), bf16)

    (spec4, sds4), (spec16, sds16) = dilated(4), dilated(16)
    tok = lambda width: pl.BlockSpec((1, PROJ_ROWS, width), lambda bi, i: (bi, i, 0))
    return pl.pallas_call(
        _proj_attn_kernel,
        grid=(b, s // PROJ_ROWS),
        in_specs=[tok(D_MODEL), _resident((1, D_MODEL)),
                  _resident(w_d1.shape), _resident(w_d4.shape), _resident(w_d16.shape)],
        out_specs=[tok(ATTN_QKV_WIDTH), spec4, spec16],
        out_shape=[jax.ShapeDtypeStruct((b, s, ATTN_QKV_WIDTH), bf16), sds4, sds16],
        scratch_shapes=[pltpu.VMEM((D_MODEL // LANES, PROJ_ROWS, LANES), jnp.float32),
                        pltpu.VMEM((3, PROJ_ROWS, D_MODEL), jnp.bfloat16)],
        compiler_params=pltpu.CompilerParams(
            dimension_semantics=("arbitrary", "arbitrary"), vmem_limit_bytes=VMEM_LIMIT),
        name="proj_attn",
    )(x1, gm, w_d1, w_d4, w_d16)


def _attn_kernel(q_ref, kp_ref, kc_ref, kn_ref, vp_ref, vc_ref, vn_ref, o_ref, lse_ref,
                 *, dilation, sub_len, tq):
    q0 = pl.program_id(1) * tq
    kwin = jnp.concatenate([kp_ref[0], kc_ref[0], kn_ref[0]], axis=0)
    vwin = jnp.concatenate([vp_ref[0], vc_ref[0], vn_ref[0]], axis=0)

    row = lax.broadcasted_iota(jnp.int32, (ATTN_SUB, ATTN_WIN), 0)
    col = lax.broadcasted_iota(jnp.int32, (ATTN_SUB, ATTN_WIN), 1)
    rel = jnp.abs(col - HALF_TAPS - row)
    lane = lax.broadcasted_iota(jnp.int32, (ATTN_SUB, 2 * ATTN_HEAD_DIM), 1)
    first_head = lane < ATTN_HEAD_DIM
    ones = jnp.ones((ATTN_WIN, 2 * ATTN_HEAD_DIM), jnp.bfloat16)

    for sub in range(tq // ATTN_SUB):
        r0 = sub * ATTN_SUB
        kpos = q0 + (r0 - HALF_TAPS) + col
        valid = (rel <= HALF_TAPS) & (kpos >= 0) & (kpos < sub_len)
        base = jnp.where(valid, (-dilation) * rel.astype(jnp.float32), MASK_BIAS)
        lse_packed = None
        for pair in range(ATTN_HEADS // 2):
            lanes = slice(pair * LANES, (pair + 1) * LANES)
            qp = q_ref[0, r0:r0 + ATTN_SUB, lanes]
            kw = kwin[r0:r0 + ATTN_WIN, lanes]
            vw = vwin[r0:r0 + ATTN_WIN, lanes]
            zero = jnp.zeros_like(qp)
            q2 = jnp.concatenate(
                [jnp.where(first_head, qp, zero), jnp.where(first_head, zero, qp)], axis=0)
            s = lax.dot_general(q2, kw, _NT, preferred_element_type=jnp.float32)
            halves = []
            for half in range(2):
                slope = 2.0 ** (-(2 * pair + half + 1))
                sh = s[half * ATTN_SUB:(half + 1) * ATTN_SUB] + slope * base
                m = jnp.max(sh, axis=-1, keepdims=True)
                halves.append((m, jnp.exp((sh - m).astype(jnp.bfloat16))))
            p2 = jnp.concatenate([halves[0][1], halves[1][1]], axis=0)
            vext = jnp.concatenate([vw, ones], axis=1)
            o2 = jnp.dot(p2, vext, preferred_element_type=jnp.float32)
            outs = []
            for half in range(2):
                rows = slice(half * ATTN_SUB, (half + 1) * ATTN_SUB)
                num = o2[rows, :LANES]
                den = o2[rows, LANES:]
                outs.append(num / den)
                lse = halves[half][0] + jnp.log(den)
                head = 2 * pair + half
                lse_packed = lse if head == 0 else jnp.where(
                    lane >= head * LSE_LANES, lse, lse_packed)
            o_ref[0, r0:r0 + ATTN_SUB, lanes] = jnp.where(
                first_head, outs[0], outs[1]).astype(o_ref.dtype)
        lse_ref[0, r0:r0 + ATTN_SUB, :] = lse_packed


def _attention(qkv, col0, dilation):
    nsub, sub_len, _ = qkv.shape
    tq = next(t for t in (1024, 512, 256, ATTN_SUB) if sub_len % t == 0)
    halo_per_tile = tq // HALF_TAPS
    last_halo = sub_len // HALF_TAPS - 1

    def cur(which):
        return lambda si, i: (si, i, col0 + which)

    def prev(which):
        return lambda si, i: (si, jnp.maximum(i * halo_per_tile - 1, 0), col0 + which)

    def nxt(which):
        return lambda si, i: (si, jnp.minimum((i + 1) * halo_per_tile, last_halo), col0 + which)

    main = lambda which: pl.BlockSpec((1, tq, ATTN_GROUP_WIDTH), cur(which))
    halo = lambda fn, which: pl.BlockSpec((1, HALF_TAPS, ATTN_GROUP_WIDTH), fn(which))
    out = lambda width: pl.BlockSpec((1, tq, width), lambda si, i: (si, i, 0))
    return pl.pallas_call(
        functools.partial(_attn_kernel, dilation=dilation, sub_len=sub_len, tq=tq),
        grid=(nsub, sub_len // tq),
        in_specs=[main(0), halo(prev, 1), main(1), halo(nxt, 1), halo(prev, 2), main(2), halo(nxt, 2)],
        out_specs=[out(ATTN_GROUP_WIDTH), out(LANES)],
        out_shape=[jax.ShapeDtypeStruct((nsub, sub_len, ATTN_GROUP_WIDTH), jnp.bfloat16),
                   jax.ShapeDtypeStruct((nsub, sub_len, LANES), jnp.float32)],
        compiler_params=pltpu.CompilerParams(
            dimension_semantics=("arbitrary", "arbitrary"), vmem_limit_bytes=VMEM_LIMIT),
        name=f"attn_d{dilation}",
    )(qkv, qkv, qkv, qkv, qkv, qkv, qkv)


def _hgrn_tile_stages(rows, tri_ref, qs_ref, gate, v_ref, carry, o_ref, *, reverse, finish=None):
    f32, bf16 = jnp.float32, jnp.bfloat16
    lf_hi_ref, lf_mid_ref, k_ref = gate
    parts = jnp.dot(tri_ref[...],
                    jnp.concatenate([lf_hi_ref[0, rows, :], lf_mid_ref[0, rows, :]], axis=1),
                    preferred_element_type=f32)
    yield
    bcum = parts[:, HGRN_DIM:] + parts[:, :HGRN_DIM]
    mid, last = (HGRN_TILE // 2, 0) if reverse else (HGRN_TILE // 2 - 1, HGRN_TILE - 1)
    b_mid = bcum[mid:mid + 1]
    b_last = bcum[last:last + 1]
    decay = jnp.exp2(bcum - b_mid)
    q_t = (qs_ref[0, rows, :].astype(f32) * decay).astype(bf16)
    k_mid = k_ref[0, rows, :].astype(f32) / decay
    k_end = (k_mid * jnp.exp2(b_last - b_mid)).astype(bf16)
    v = v_ref[0, rows, :]
    a = lax.dot_general(q_t, k_mid.astype(bf16), _NT, preferred_element_type=f32)
    update = lax.dot_general(v, k_end, _TN, preferred_element_type=f32)
    yield
    state = carry["state"]
    inter = lax.dot_general(q_t, (state * jnp.exp2(b_mid)).astype(bf16), _NT,
                            preferred_element_type=f32)
    carry["state"] = state * jnp.exp2(b_last) + update
    tpos = lax.broadcasted_iota(jnp.int32, (HGRN_TILE, HGRN_TILE), 0)
    spos = lax.broadcasted_iota(jnp.int32, (HGRN_TILE, HGRN_TILE), 1)
    seen = (tpos <= spos) if reverse else (tpos >= spos)
    a = jnp.where(seen, a, 0.0).astype(bf16)
    intra = jnp.dot(a, v, preferred_element_type=f32)
    yield
    if finish is None:
        o_ref[rows, :] = intra + inter
    else:
        finish(rows, intra + inter + o_ref[rows, :])


def _hgrn_kernel(qs_ref, fhi_ref, fmid_ref, fk_ref, bhi_ref, bmid_ref, bk_ref, v_ref, hg_ref,
                 ng_ref, trif_ref, trib_ref, out_ref, of_ref, ob_ref, stf_ref, stb_ref, *, seq):
    n_tiles = seq // HGRN_TILE
    stf_ref[...] = jnp.zeros_like(stf_ref)
    stb_ref[...] = jnp.zeros_like(stb_ref)
    fwd_gate = (fhi_ref, fmid_ref, fk_ref)
    bwd_gate = (bhi_ref, bmid_ref, bk_ref)

    def finish(rows, o):
        o = _rms(o, ng_ref[...]) * hg_ref[0, rows, :].astype(jnp.float32)
        out_ref[0, rows, :] = o.astype(jnp.bfloat16)

    def step(t, loop_carry, *, second_half):
        fwd = {"state": stf_ref[...]}
        bwd = {"state": stb_ref[...]}
        tiles = []
        for j in range(HGRN_UNROLL):
            tf = t * HGRN_UNROLL + j
            fwd_rows = pl.ds(pl.multiple_of(tf * HGRN_TILE, HGRN_TILE), HGRN_TILE)
            bwd_rows = pl.ds(pl.multiple_of((n_tiles - 1 - tf) * HGRN_TILE, HGRN_TILE), HGRN_TILE)
            tiles.append(_hgrn_tile_stages(
                fwd_rows, trif_ref, qs_ref, fwd_gate, v_ref, fwd, ob_ref if second_half else of_ref,
                reverse=False, finish=finish if second_half else None))
            tiles.append(_hgrn_tile_stages(
                bwd_rows, trib_ref, qs_ref, bwd_gate, v_ref, bwd, of_ref if second_half else ob_ref,
                reverse=True, finish=finish if second_half else None))
        while tiles:
            tiles = [g for g in tiles if next(g, _DONE) is not _DONE]
        stf_ref[...] = fwd["state"]
        stb_ref[...] = bwd["state"]
        return loop_carry

    n_steps = n_tiles // HGRN_UNROLL
    lax.fori_loop(0, n_steps // 2, functools.partial(step, second_half=False), 0)
    lax.fori_loop(n_steps // 2, n_steps, functools.partial(step, second_half=True), 0)


def _tile_triangles():
    r = jnp.arange(HGRN_TILE)[:, None]
    c = jnp.arange(HGRN_TILE)[None, :]
    return (r >= c).astype(jnp.bfloat16), (r <= c).astype(jnp.bfloat16)


def _hgrn(p3d, norm_g):
    b, s, _ = p3d.shape
    assert s % (2 * HGRN_UNROLL * HGRN_TILE) == 0
    tri_f, tri_b = _tile_triangles()

    def head_cols(col):
        return pl.BlockSpec((1, s, HGRN_DIM), lambda bi, h: (bi, 0, col // HGRN_DIM + h))

    planes = [COL_QS] + [c0 + i * D_MODEL for c0 in (COL_FWD, COL_BWD) for i in range(3)] + [
        COL_HI, COL_HG]
    return pl.pallas_call(
        functools.partial(_hgrn_kernel, seq=s),
        grid=(b, HGRN_HEADS),
        in_specs=[head_cols(col) for col in planes] + [
            pl.BlockSpec((1, HGRN_DIM), lambda bi, h: (0, h)),
            _resident(tri_f.shape), _resident(tri_b.shape)],
        out_specs=pl.BlockSpec((1, s, HGRN_DIM), lambda bi, h: (bi, 0, h)),
        out_shape=jax.ShapeDtypeStruct((b, s, HGRN_HEADS * HGRN_DIM), jnp.bfloat16),
        scratch_shapes=[
            pltpu.VMEM((s, HGRN_DIM), jnp.float32),
            pltpu.VMEM((s, HGRN_DIM), jnp.float32),
            pltpu.VMEM((HGRN_DIM, HGRN_DIM), jnp.float32),
            pltpu.VMEM((HGRN_DIM, HGRN_DIM), jnp.float32),
        ],
        compiler_params=pltpu.CompilerParams(
            dimension_semantics=("arbitrary", "arbitrary"), vmem_limit_bytes=VMEM_LIMIT),
        name="hgrn2",
    )(*([p3d] * len(planes)), norm_g, tri_f, tri_b)


def _mix_kernel(x1_ref, gates_ref, o0_ref, l0_ref, o4_ref, l4_ref, o16_ref, l16_ref, yb_ref,
                widen_ref, wa_ref, wb_ref, wo_ref, x2_ref, so4_ref, sl4_ref, so16_ref, sl16_ref):
    f32, bf16 = jnp.float32, jnp.bfloat16
    for d, pairs in ((4, ((o4_ref, so4_ref), (l4_ref, sl4_ref))),
                     (16, ((o16_ref, so16_ref), (l16_ref, sl16_ref)))):
        n = MIX_ROWS // d
        for src_ref, dst_ref in pairs:
            for r in range(d):
                rows = src_ref[0, r].astype(f32)
                for c in range(dst_ref.shape[0]):
                    dst_ref[c, pl.ds(r, n, stride=d), :] = rows[:, c * LANES:(c + 1) * LANES]
    tokens = lambda ref: jnp.concatenate([ref[c] for c in range(ref.shape[0])], axis=1)
    l0, l1, l2 = l0_ref[0], sl4_ref[0], sl16_ref[0]
    m = jnp.maximum(jnp.maximum(l0, l1), l2)
    e0, e1, e2 = jnp.exp(l0 - m), jnp.exp(l1 - m), jnp.exp(l2 - m)
    inv = 1.0 / (e0 + e1 + e2)

    def widen(w):
        hi = w.astype(bf16)
        mid = (w - hi.astype(f32)).astype(bf16)
        return jnp.dot(jnp.concatenate([hi, mid], axis=1), widen_ref[...],
                       preferred_element_type=f32)

    ya = (widen(e0 * inv) * o0_ref[0].astype(f32) + widen(e1 * inv) * tokens(so4_ref)
          + widen(e2 * inv) * tokens(so16_ref))
    ma = jnp.dot(ya.astype(bf16), wa_ref[...], preferred_element_type=f32)
    mb = jnp.dot(yb_ref[0], wb_ref[...], preferred_element_type=f32)
    ga = gates_ref[0, :, :D_MODEL].astype(f32)
    gb = gates_ref[0, :, D_MODEL:].astype(f32)
    merged = jax.nn.sigmoid(ga) * ma + jax.nn.sigmoid(gb) * mb
    x2_ref[0] = x1_ref[0] + jnp.dot(merged.astype(bf16), wo_ref[...], preferred_element_type=f32)


def _widen_matrix():
    k = jnp.arange(2 * LANES)[:, None] % LANES
    head = jnp.arange(ATTN_GROUP_WIDTH)[None, :] // ATTN_HEAD_DIM
    return (k == head * LSE_LANES).astype(jnp.bfloat16)


def _mix(x1, pm, attn0, attn4, attn16, yb, wa, wb, wo):
    b, s, _ = x1.shape
    widen = _widen_matrix()
    tok = lambda width: pl.BlockSpec((1, MIX_ROWS, width), lambda bi, i: (bi, i, 0))
    sub = lambda d, width: pl.BlockSpec((1, d, MIX_ROWS // d, width), lambda bi, i: (bi, 0, i, 0))
    planes = lambda width: pltpu.VMEM((width // LANES, MIX_ROWS, LANES), jnp.float32)
    return pl.pallas_call(
        _mix_kernel,
        grid=(b, s // MIX_ROWS),
        in_specs=[tok(D_MODEL), tok(2 * D_MODEL), tok(ATTN_GROUP_WIDTH), tok(LANES),
                  sub(4, ATTN_GROUP_WIDTH), sub(4, LANES), sub(16, ATTN_GROUP_WIDTH), sub(16, LANES),
                  tok(D_MODEL), _resident(widen.shape),
                  _resident(wa.shape), _resident(wb.shape), _resident(wo.shape)],
        out_specs=tok(D_MODEL),
        out_shape=jax.ShapeDtypeStruct((b, s, D_MODEL), jnp.float32),
        scratch_shapes=[planes(ATTN_GROUP_WIDTH), planes(LANES)] * 2,
        compiler_params=pltpu.CompilerParams(
            dimension_semantics=("arbitrary", "arbitrary"), vmem_limit_bytes=VMEM_LIMIT),
        name="mix",
    )(x1, pm, *attn0, *attn4, *attn16, yb, widen, wa, wb, wo)


def _ffn_weights(w_gu, w_down):
    bf16 = jnp.bfloat16
    return w_gu[:, :D_FF].astype(bf16), w_gu[:, D_FF:].astype(bf16), w_down.astype(bf16)


def _in_weights(w_in):
    aw = 3 * ATTN_GROUP_WIDTH
    a_q, a_k, a_v = w_in[:, :aw], w_in[:, aw:2 * aw], w_in[:, 2 * aw:3 * aw]
    rest = w_in[:, 3 * aw:]
    h_q, h_ff, h_fb, h_i, h_g, g_a, g_b = jnp.split(rest, 7, axis=1)
    q_scale = 1.0 / math.sqrt(ATTN_HEAD_DIM)

    def qkv(g):
        sl = slice(g * ATTN_GROUP_WIDTH, (g + 1) * ATTN_GROUP_WIDTH)
        return [a_q[:, sl] * q_scale, a_k[:, sl], a_v[:, sl]]

    cat = lambda cols: jnp.concatenate(cols, axis=1).astype(jnp.bfloat16)
    return cat([g_a, g_b, h_q, h_ff, h_fb, h_i, h_g]), cat(qkv(0)), cat(qkv(1)), cat(qkv(2))


def _lower_bound(lb_raw, layer):
    return jnp.cumsum(jax.nn.softmax(lb_raw.astype(jnp.float32), axis=0), axis=0)[layer][None, :]


def _trunk(x, params):
    (g1, gm, ffn1_w, w_in, lb_f, lb_b, hgrn_norm, wa, wb, wo, g2, gf, ffn2_w) = params
    b, s, _ = x.shape
    n = b * s
    x1 = _ffn(x.reshape(n, D_MODEL), g1, ffn1_w).reshape(b, s, D_MODEL)
    pm = _proj_main(x1, gm, lb_f, lb_b, w_in[0])
    p1, p4, p16 = _proj_attn(x1, gm, *w_in[1:])
    attn0 = _attention(p1, 0, 1)
    dilated = []
    for d, qkv in ((4, p4), (16, p16)):
        o, lse = _attention(qkv.reshape(b * d, s // d, ATTN_QKV_WIDTH), 0, d)
        dilated.append((o.reshape(b, d, s // d, ATTN_GROUP_WIDTH),
                        lse.reshape(b, d, s // d, LANES)))
    yb = _hgrn(pm, hgrn_norm)
    x2 = _mix(x1, pm, attn0, *dilated, yb, wa, wb, wo)
    y = _ffn(x2.reshape(n, D_MODEL), g2, ffn2_w, gout=gf)
    return y.reshape(b, s, D_MODEL)


def kernel(x_prompt, x_sample, ffn1_norm, ffn1_w_gu, ffn1_w_down, mix_norm, w_in, hgrn_lb_fwd,
           hgrn_lb_bwd, hgrn_norm, w_branch_a, w_branch_b, w_out, ffn2_norm, ffn2_w_gu,
           ffn2_w_down, final_norm):
    bf16 = jnp.bfloat16
    layer = 0
    params = (
        ffn1_norm[layer][None, :], mix_norm[layer][None, :],
        _ffn_weights(ffn1_w_gu[layer], ffn1_w_down[layer]),
        _in_weights(w_in[layer]),
        _lower_bound(hgrn_lb_fwd, layer), _lower_bound(hgrn_lb_bwd, layer),
        hgrn_norm[layer][None, :],
        w_branch_a[layer].astype(bf16), w_branch_b[layer].astype(bf16), w_out[layer].astype(bf16),
        ffn2_norm[layer][None, :], final_norm[None, :],
        _ffn_weights(ffn2_w_gu[layer], ffn2_w_down[layer]),
    )
    return (_trunk(x_prompt, params), _trunk(x_sample, params))
```
